```python
import math
import jax
import jax.numpy as jnp
from jax import lax
import numpy as np

D_MODEL = 2048
BATCH = 32
SEQ = 256
DEPTH = 2
DEC_BATCH = 4
DEC_SEQ = 4096
PAST_LEN = 256

GRID_W = 64
EPS = 1e-6
NEG_INF = -1e30
N_BRANCH = 4
SHORT_CONV = 3
DN_HEADS = 4
DN_DK = 128
DN_DV = 128
DN_CHUNK = 64
SSD_HEADS = 8
SSD_P = 64
SSD_GROUPS = 2
SSD_N = 128
SSD_CHUNK = 64
HY_WIDTH = 512
HY_BANDS = 16
HY_EMB = 2 * HY_BANDS + 1
HY_HID = 64
NA_HEADS = 8
NA_HD = 64
NA_WIN_R = 8
NA_WIN_C = 16
BRANCH_W = 512
N_EXPERTS = 64
TOP_K = 6
N_EXP_GROUPS = 8
TOPK_GROUPS = 4
EXPERT_FF = 512
SHARED_FF = 512
ROUTED_SCALE = 2.5
MOE_BLOCK = 128

DN_QKV = 2 * DN_HEADS * DN_DK + DN_HEADS * DN_DV
SSD_XBC = SSD_HEADS * SSD_P + 2 * SSD_GROUPS * SSD_N
IN_SPLITS = (DN_QKV, DN_HEADS * DN_DV, 2 * DN_HEADS, 2 * DN_HEADS,
             SSD_HEADS * SSD_P, SSD_XBC, 2 * SSD_HEADS,
             3 * HY_WIDTH, 3 * NA_HEADS * NA_HD, N_BRANCH * D_MODEL)
N_IN = sum(IN_SPLITS)

kernel_name = "hybrid_diffusion_prefix_trunk_step"


def rmsnorm(x, w):
    xf = x.astype(jnp.float32)
    y = xf * lax.rsqrt(jnp.mean(xf * xf, axis=-1, keepdims=True) + EPS)
    return (y * w.astype(jnp.float32)).astype(x.dtype)


def l2norm(x):
    return x * lax.rsqrt(jnp.sum(x * x, axis=-1, keepdims=True) + EPS)


def split_cols(h, sizes):
    return jnp.split(h, np.cumsum(sizes)[:-1].tolist(), axis=-1)


def flip_seq(t):
    return jnp.flip(t, axis=1)


def dwconv_centred(u, w, b):
    k = w.shape[0]
    pad = k // 2
    length = u.shape[1]
    up = jnp.pad(u, ((0, 0), (pad, pad), (0, 0)))
    y = up[:, 0:length] * w[0]
    for j in range(1, k):
        y = y + up[:, j:j + length] * w[j]
    return y + b


def to_chunks(t, chunk):
    b, length, h = t.shape[:3]
    t = t.reshape((b, length // chunk, chunk, h) + t.shape[3:])
    return jnp.moveaxis(jnp.moveaxis(t, 1, 0), 3, 2)


def from_chunks(o):
    nc, b, h, c, xdim = o.shape
    return o.transpose(1, 0, 3, 2, 4).reshape(b, nc * c, h, xdim)


def tri_masks(c):
    idx = jnp.arange(c)
    return idx[:, None] >= idx[None, :], idx[:, None] > idx[None, :]


def gated_delta_chunked(q, k, v, g, beta, s0):
    lower, strict = tri_masks(DN_CHUNK)
    eye = jnp.eye(DN_CHUNK, dtype=jnp.float32)
    dv = v.shape[-1]

    def body(s, inp):
        qc, kc, vc, gc, bc = inp
        gcum = jnp.cumsum(gc, axis=-1)
        decay = jnp.exp(jnp.where(lower, gcum[..., :, None] - gcum[..., None, :], -jnp.inf))
        kb = kc * bc[..., None]
        a_mat = eye + jnp.where(strict, jnp.einsum("bhik,bhjk->bhij", kb, kc) * decay, 0.0)
        rhs = jnp.concatenate([vc * bc[..., None], kb * jnp.exp(gcum)[..., None]], axis=-1)
        sol = lax.linalg.triangular_solve(a_mat, rhs, left_side=True, lower=True, unit_diagonal=True)
        u, w = sol[..., :dv], sol[..., dv:]
        v_new = u - jnp.einsum("bhck,bhkv->bhcv", w, s)
        attn = jnp.einsum("bhik,bhjk->bhij", qc, kc) * decay
        o = (jnp.einsum("bhck,bhkv->bhcv", qc * jnp.exp(gcum)[..., None], s)
             + jnp.einsum("bhij,bhjv->bhiv", attn, v_new))
        glast = gcum[..., -1:]
        s_new = (s * jnp.exp(glast)[..., None]
                 + jnp.einsum("bhck,bhcv->bhkv", kc * jnp.exp(glast - gcum)[..., None], v_new))
        return s_new, o

    s_fin, o = lax.scan(body, s0, tuple(to_chunks(t, DN_CHUNK) for t in (q, k, v, g, beta)))
    return from_chunks(o), s_fin


def ssd_chunked(x, dt, bh, ch, a_diag, s0):
    lower, _ = tri_masks(SSD_CHUNK)

    def body(s, inp):
        xc, dtc, bc, cc = inp
        acs = jnp.cumsum(dtc * a_diag[None, :, None], axis=-1)
        decay = jnp.exp(jnp.where(lower, acs[..., :, None] - acs[..., None, :], -jnp.inf))
        xdt = xc * dtc[..., None]
        scores = jnp.einsum("bhin,bhjn->bhij", cc, bc) * decay
        y = (jnp.einsum("bhij,bhjp->bhip", scores, xdt)
             + jnp.einsum("bhin,bhpn->bhip", cc, s) * jnp.exp(acs)[..., None])
        alast = acs[..., -1:]
        s_new = (s * jnp.exp(alast)[..., None]
                 + jnp.einsum("bhjp,bhjn->bhpn", xdt * jnp.exp(alast - acs)[..., None], bc))
        return s_new, y

    s_fin, y = lax.scan(body, s0, tuple(to_chunks(t, SSD_CHUNK) for t in (x, dt, bh, ch)))
    return from_chunks(y), s_fin


def deltanet_mixer(qkv, z, b_raw, a_raw, conv_w, conv_b, a_log, dt_bias, norm_w, s0):
    bsz, length, _ = qkv.shape
    qkv = jax.nn.silu(dwconv_centred(qkv, conv_w, conv_b)).astype(jnp.float32)
    q, k, v = jnp.split(qkv, [DN_HEADS * DN_DK, 2 * DN_HEADS * DN_DK], axis=-1)
    q = l2norm(q.reshape(bsz, length, DN_HEADS, DN_DK)) * (DN_DK ** -0.5)
    k = l2norm(k.reshape(bsz, length, DN_HEADS, DN_DK))
    v = v.reshape(bsz, length, DN_HEADS, DN_DV)
    b_raw = b_raw.astype(jnp.float32)
    a_raw = a_raw.astype(jnp.float32)
    outs, finals = [], []
    for d in range(2):
        sl = slice(d * DN_HEADS, (d + 1) * DN_HEADS)
        beta = jax.nn.sigmoid(b_raw[..., sl])
        g = -jnp.exp(a_log[d].astype(jnp.float32)) * jax.nn.softplus(a_raw[..., sl] + dt_bias[d].astype(jnp.float32))
        seq = (q, k, v, g, beta) if d == 0 else tuple(flip_seq(t) for t in (q, k, v, g, beta))
        o, s_fin = gated_delta_chunked(*seq, s0[:, d].astype(jnp.float32))
        outs.append(o if d == 0 else flip_seq(o))
        finals.append(s_fin)
    o = rmsnorm(outs[0] + outs[1], norm_w) * jax.nn.silu(z.astype(jnp.float32).reshape(bsz, length, DN_HEADS, DN_DV))
    return o.reshape(bsz, length, DN_HEADS * DN_DV).astype(z.dtype), jnp.stack(finals, axis=1)


def ssd_mixer(z, xbc, dt_raw, conv_w, conv_b, a_log, dt_bias, d_skip, norm_w, s0):
    bsz, length, _ = xbc.shape
    xbc = jax.nn.silu(dwconv_centred(xbc, conv_w, conv_b)).astype(jnp.float32)
    x, bm, cm = jnp.split(xbc, [SSD_HEADS * SSD_P, SSD_HEADS * SSD_P + SSD_GROUPS * SSD_N], axis=-1)
    x = x.reshape(bsz, length, SSD_HEADS, SSD_P)
    rep = SSD_HEADS // SSD_GROUPS
    bh = jnp.repeat(bm.reshape(bsz, length, SSD_GROUPS, SSD_N), rep, axis=2)
    ch = jnp.repeat(cm.reshape(bsz, length, SSD_GROUPS, SSD_N), rep, axis=2)
    dt_raw = dt_raw.astype(jnp.float32)
    outs, finals = [], []
    for d in range(2):
        dt = jax.nn.softplus(dt_raw[..., d * SSD_HEADS:(d + 1) * SSD_HEADS] + dt_bias[d].astype(jnp.float32))
        a_diag = -jnp.exp(a_log[d].astype(jnp.float32))
        seq = (x, dt, bh, ch) if d == 0 else tuple(flip_seq(t) for t in (x, dt, bh, ch))
        y, s_fin = ssd_chunked(*seq, a_diag, s0[:, d].astype(jnp.float32))
        outs.append(y if d == 0 else flip_seq(y))
        finals.append(s_fin)
    y = outs[0] + outs[1] + x * d_skip.astype(jnp.float32)[:, None]
    y = rmsnorm(y.reshape(bsz, length, SSD_HEADS * SSD_P) * jax.nn.silu(z.astype(jnp.float32)), norm_w)
    return y.astype(z.dtype), jnp.stack(finals, axis=1)


def hyena_filter(length, f1_w, f1_b, f2_w, f2_b, f3_w, freq, decay):
    off = (jnp.arange(length, dtype=jnp.float32) - (length // 2)) / length
    ang = 2.0 * math.pi * off[:, None] * jnp.arange(1, HY_BANDS + 1, dtype=jnp.float32)
    feat = jnp.concatenate([off[:, None], jnp.sin(ang), jnp.cos(ang)], axis=-1)
    fq = freq.astype(jnp.float32)
    hdn = jnp.sin(fq * (feat @ f1_w.astype(jnp.float32) + f1_b.astype(jnp.float32)))
    hdn = jnp.sin(fq * (hdn @ f2_w.astype(jnp.float32) + f2_b.astype(jnp.float32)))
    filt = hdn @ f3_w.astype(jnp.float32)
    window = jnp.exp(-jnp.abs(decay.astype(jnp.float32)) * (2.0 * jnp.abs(off))[:, None])
    return filt * window


def centred_long_conv(u, filt):
    length = u.shape[1]
    n = 2 * length
    uf = jnp.fft.rfft(u, n=n, axis=1)
    hf = jnp.fft.rfft(filt, n=n, axis=0)
    y = jnp.fft.irfft(uf * hf[None], n=n, axis=1)
    return y[:, length // 2: length // 2 + length]


def hyena_mixer(u, conv_w, conv_b, f1_w, f1_b, f2_w, f2_b, f3_w, freq, decay, bias):
    length = u.shape[1]
    uc = dwconv_centred(u, conv_w, conv_b)
    x0, x1, v = jnp.split(uc, 3, axis=-1)
    filt = hyena_filter(length, f1_w, f1_b, f2_w, f2_b, f3_w, freq, decay)
    v = (v * x1).astype(jnp.float32)
    y = centred_long_conv(v, filt) + v * bias.astype(jnp.float32)
    return (y * x0.astype(jnp.float32)).astype(u.dtype)


def context_attention(q, k, v):
    bsz, length = q.shape[:2]
    s = jnp.einsum("bqhd,bkhd->bhqk", q, k).astype(jnp.float32) * (NA_HD ** -0.5)
    p = jax.nn.softmax(s, axis=-1).astype(v.dtype)
    return jnp.einsum("bhqk,bkhd->bqhd", p, v).reshape(bsz, length, NA_HEADS * NA_HD)


def neighbourhood_attention(q, k, v, k_ctx, v_ctx, rpb):
    bsz, length = q.shape[:2]
    rows_n = length // GRID_W
    wr = min(NA_WIN_R, rows_n)
    rows = jnp.arange(rows_n)
    row_start = jnp.clip(rows - wr // 2, 0, rows_n - wr)
    krow = row_start[:, None] + jnp.arange(wr)[None, :]
    cols = jnp.arange(GRID_W)
    col_start = jnp.clip(cols - NA_WIN_C // 2, 0, GRID_W - NA_WIN_C)
    col_in = (cols[None, :] >= col_start[:, None]) & (cols[None, :] < col_start[:, None] + NA_WIN_C)
    qg = q.reshape(bsz, rows_n, GRID_W, NA_HEADS, NA_HD)
    kg = k.reshape(bsz, rows_n, GRID_W, NA_HEADS, NA_HD)[:, krow]
    vg = v.reshape(bsz, rows_n, GRID_W, NA_HEADS, NA_HD)[:, krow]
    scale = NA_HD ** -0.5
    s_loc = jnp.einsum("brqhd,brikhd->bhrqik", qg, kg).astype(jnp.float32) * scale
    dr = krow - rows[:, None] + (NA_WIN_R - 1)
    dc = jnp.clip(cols[None, :] - cols[:, None], -(NA_WIN_C - 1), NA_WIN_C - 1) + (NA_WIN_C - 1)
    bias = rpb.astype(jnp.float32)[:, dr[:, None, :, None], dc[None, :, None, :]]
    s_loc = jnp.where(col_in[None, None, None, :, None, :], s_loc + bias[None], NEG_INF)
    s_loc = s_loc.reshape(bsz, NA_HEADS, rows_n, GRID_W, wr * GRID_W)
    s_ctx = jnp.einsum("brqhd,bchd->bhrqc", qg, k_ctx).astype(jnp.float32) * scale
    p = jax.nn.softmax(jnp.concatenate([s_loc, s_ctx], axis=-1), axis=-1).astype(v.dtype)
    p_loc = p[..., :wr * GRID_W].reshape(bsz, NA_HEADS, rows_n, GRID_W, wr, GRID_W)
    p_ctx = p[..., wr * GRID_W:]
    o = (jnp.einsum("bhrqik,brikhd->brqhd", p_loc, vg)
         + jnp.einsum("bhrqc,bchd->brqhd", p_ctx, v_ctx))
    return o.reshape(bsz, length, NA_HEADS * NA_HD)


def swiglu(x, w1, w3, w2):
    return (jax.nn.silu(x @ w1) * (x @ w3)) @ w2


def grouped_experts(xt, eidx, wts, w1, w3, w2):
    t_n, d = xt.shape
    n_pairs = t_n * TOP_K
    flat_e = eidx.reshape(-1)
    flat_tok = jnp.repeat(jnp.arange(t_n, dtype=jnp.int32), TOP_K)
    flat_w = wts.reshape(-1)
    counts = jnp.zeros((N_EXPERTS,), jnp.int32).at[flat_e].add(1)
    cnt_start = jnp.cumsum(counts) - counts
    pad_counts = (counts + MOE_BLOCK - 1) // MOE_BLOCK * MOE_BLOCK
    pad_end = jnp.cumsum(pad_counts)
    pad_start = pad_end - pad_counts
    order = jnp.argsort(flat_e, stable=True)
    se = flat_e[order]
    dest = pad_start[se] + jnp.arange(n_pairs, dtype=jnp.int32) - cnt_start[se]
    n_blocks = -(-n_pairs // MOE_BLOCK) + N_EXPERTS
    cap = n_blocks * MOE_BLOCK
    buf_tok = jnp.full((cap,), t_n, jnp.int32).at[dest].set(flat_tok[order])
    buf_w = jnp.zeros((cap,), jnp.float32).at[dest].set(flat_w[order])
    blk_e = jnp.minimum(jnp.searchsorted(pad_end, jnp.arange(n_blocks, dtype=jnp.int32) * MOE_BLOCK, side="right"),
                        N_EXPERTS - 1)
    x_pad = jnp.concatenate([xt, jnp.zeros((1, d), xt.dtype)], axis=0)

    def body(acc, inp):
        tok, w, e = inp
        xb = x_pad[tok]
        yb = swiglu(xb, w1[e], w3[e], w2[e]) * w[:, None].astype(xb.dtype)
        return acc.at[tok].add(yb), None

    acc, _ = lax.scan(body, jnp.zeros((t_n + 1, d), xt.dtype),
                      (buf_tok.reshape(n_blocks, MOE_BLOCK), buf_w.reshape(n_blocks, MOE_BLOCK), blk_e))
    return acc[:t_n]


def moe_ffn(h, router_w, router_bias, w1, w3, w2, sw1, sw3, sw2):
    bsz, length, d = h.shape
    xt = h.reshape(-1, d)
    t_n = xt.shape[0]
    scores = jax.nn.sigmoid((xt @ router_w).astype(jnp.float32))
    biased = scores + router_bias.astype(jnp.float32)
    per_group = N_EXPERTS // N_EXP_GROUPS
    grp_score = lax.top_k(biased.reshape(t_n, N_EXP_GROUPS, per_group), 2)[0].sum(-1)
    _, gidx = lax.top_k(grp_score, TOPK_GROUPS)
    gmask = jax.nn.one_hot(gidx, N_EXP_GROUPS, dtype=jnp.float32).sum(1) > 0
    masked = jnp.where(jnp.repeat(gmask, per_group, axis=-1), biased, NEG_INF)
    _, eidx = lax.top_k(masked, TOP_K)
    wts = jnp.take_along_axis(scores, eidx, axis=-1)
    wts = wts / jnp.sum(wts, axis=-1, keepdims=True) * ROUTED_SCALE
    routed = grouped_experts(xt, eidx.astype(jnp.int32), wts, w1, w3, w2)
    return (routed + swiglu(xt, sw1, sw3, sw2)).reshape(bsz, length, d)


def mixer_block(h, l, prm, ctx_k, ctx_v, dn_s0, ssd_s0, is_ctx):
    bsz, length, _ = h.shape
    (dn_qkv, dn_z, dn_b, dn_a, ssd_z, ssd_xbc, ssd_dt, hy_u, na_qkv, gate_logits) = split_cols(h @ prm["w_in"][l], IN_SPLITS)
    o_a, dn_fin = deltanet_mixer(dn_qkv, dn_z, dn_b, dn_a, prm["dn_conv_w"][l], prm["dn_conv_b"][l],
                                 prm["dn_A_log"][l], prm["dn_dt_bias"][l], prm["dn_norm_w"][l], dn_s0)
    o_b, ssd_fin = ssd_mixer(ssd_z, ssd_xbc, ssd_dt, prm["ssd_conv_w"][l], prm["ssd_conv_b"][l],
                             prm["ssd_A_log"][l], prm["ssd_dt_bias"][l], prm["ssd_D"][l], prm["ssd_norm_w"][l], ssd_s0)
    o_c = hyena_mixer(hy_u, prm["hy_conv_w"][l], prm["hy_conv_b"][l], prm["hy_f1_w"][l], prm["hy_f1_b"][l],
                      prm["hy_f2_w"][l], prm["hy_f2_b"][l], prm["hy_f3_w"][l], prm["hy_freq"][l],
                      prm["hy_decay"][l], prm["hy_bias"][l])
    nq, nk, nv = [t.reshape(bsz, length, NA_HEADS, NA_HD) for t in jnp.split(na_qkv, 3, axis=-1)]
    if is_ctx:
        o_d = context_attention(nq, nk, nv)
        ctx_out = (nk, nv, dn_fin, ssd_fin)
    else:
        o_d = neighbourhood_attention(nq, nk, nv, ctx_k, ctx_v, prm["na_rpb"][l])
        ctx_out = None
    gates = jnp.split(jax.nn.sigmoid(gate_logits.astype(jnp.float32)).astype(h.dtype), N_BRANCH, axis=-1)
    merged = gates[0] * (o_a @ prm["w_branch"][l, 0])
    for n, o_n in ((1, o_b), (2, o_c), (3, o_d)):
        merged = merged + gates[n] * (o_n @ prm["w_branch"][l, n])
    return merged @ prm["w_out"][l], ctx_out


def trunk_layer(x, cvec, l, prm, ctx_k, ctx_v, dn_s0, ssd_s0, is_ctx):
    mod = (jax.nn.silu(cvec) @ prm["ada_w"][l] + prm["ada_b"][l])[:, None, :]
    sh1, sc1, g1, sh2, sc2, g2 = jnp.split(mod, 6, axis=-1)
    h = rmsnorm(x, prm["norm_w"][l, 0]) * (1.0 + sc1) + sh1
    mix, ctx_out = mixer_block(h, l, prm, ctx_k, ctx_v, dn_s0, ssd_s0, is_ctx)
    x = x + g1 * mix
    h = rmsnorm(x, prm["norm_w"][l, 1]) * (1.0 + sc2) + sh2
    x = x + g2 * moe_ffn(h, prm["router_w"][l], prm["router_bias"][l], prm["exp_w1"][l], prm["exp_w3"][l],
                         prm["exp_w2"][l], prm["sh_w1"][l], prm["sh_w3"][l], prm["sh_w2"][l])
    return x, ctx_out


def _dt_bias_init(key, shape):
    dt = jnp.exp(jax.random.uniform(key, shape, jnp.float32, math.log(1e-3), math.log(1e-1)))
    return dt + jnp.log(-jnp.expm1(-dt))


def setup_inputs(seed: int = 0) -> dict:
    key = jax.random.key(seed)
    keys = iter(jax.random.split(key, 64))

    def nrm(shape, scale):
        return jax.random.normal(next(keys), shape, jnp.float32) * scale

    def gain(shape):
        return 1.0 + nrm(shape, 0.02)

    d = D_MODEL
    hy_lin = jnp.linspace(abs(math.log(1e-2)) / 1.5, abs(math.log(1e-2)) / 0.3, HY_WIDTH, dtype=jnp.float32)
    inp = {}
    inp["x_prompt"] = nrm((BATCH, SEQ, d), 1.0)
    inp["x_sample"] = nrm((DEC_BATCH, DEC_SEQ, d), 1.0)
    inp["cache_k"] = nrm((DEC_BATCH, DEPTH, PAST_LEN, NA_HEADS, NA_HD), 1.0)
    inp["cache_v"] = nrm((DEC_BATCH, DEPTH, PAST_LEN, NA_HEADS, NA_HD), 1.0)
    inp["state_delta"] = nrm((DEC_BATCH, DEPTH, 2, DN_HEADS, DN_DK, DN_DV), 0.05)
    inp["state_ssd"] = nrm((DEC_BATCH, DEPTH, 2, SSD_HEADS, SSD_P, SSD_N), 0.1)
    inp["c"] = nrm((DEC_BATCH, d), 1.0)
    inp["c_ctx"] = nrm((d,), 1.0)
    inp["ada_w"] = nrm((DEPTH, d, 6 * d), 0.5 * d ** -0.5)
    inp["ada_b"] = nrm((DEPTH, 6 * d), 0.01)
    inp["norm_w"] = gain((DEPTH, 2, d))
    inp["final_norm_w"] = gain((d,))
    inp["w_in"] = nrm((DEPTH, d, N_IN), d ** -0.5)
    inp["dn_conv_w"] = nrm((DEPTH, SHORT_CONV, DN_QKV), 0.5)
    inp["dn_conv_b"] = nrm((DEPTH, DN_QKV), 0.01)
    inp["dn_A_log"] = jnp.log(jax.random.uniform(next(keys), (DEPTH, 2, DN_HEADS), jnp.float32, 1.0, 16.0))
    inp["dn_dt_bias"] = _dt_bias_init(next(keys), (DEPTH, 2, DN_HEADS))
    inp["dn_norm_w"] = gain((DEPTH, DN_DV))
    inp["ssd_conv_w"] = nrm((DEPTH, SHORT_CONV, SSD_XBC), 0.5)
    inp["ssd_conv_b"] = nrm((DEPTH, SSD_XBC), 0.01)
    inp["ssd_A_log"] = jnp.log(jax.random.uniform(next(keys), (DEPTH, 2, SSD_HEADS), jnp.float32, 1.0, 16.0))
    inp["ssd_dt_bias"] = _dt_bias_init(next(keys), (DEPTH, 2, SSD_HEADS))
    inp["ssd_D"] = gain((DEPTH, SSD_HEADS))
    inp["ssd_norm_w"] = gain((DEPTH, SSD_HEADS * SSD_P))
    inp["hy_conv_w"] = nrm((DEPTH, SHORT_CONV, 3 * HY_WIDTH), 0.5)
    inp["hy_conv_b"] = nrm((DEPTH, 3 * HY_WIDTH), 0.01)
    inp["hy_f1_w"] = nrm((DEPTH, HY_EMB, HY_HID), HY_EMB ** -0.5)
    inp["hy_f1_b"] = nrm((DEPTH, HY_HID), 0.01)
    inp["hy_f2_w"] = nrm((DEPTH, HY_HID, HY_HID), HY_HID ** -0.5)
    inp["hy_f2_b"] = nrm((DEPTH, HY_HID), 0.01)
    inp["hy_f3_w"] = nrm((DEPTH, HY_HID, HY_WIDTH), 0.05 * HY_HID ** -0.5)
    inp["hy_freq"] = gain((DEPTH, HY_HID))
    inp["hy_decay"] = hy_lin[None, :] * (1.0 + nrm((DEPTH, HY_WIDTH), 0.05))
    inp["hy_bias"] = nrm((DEPTH, HY_WIDTH), 0.5)
    inp["na_rpb"] = nrm((DEPTH, NA_HEADS, 2 * NA_WIN_R - 1, 2 * NA_WIN_C - 1), 0.1)
    inp["w_branch"] = nrm((DEPTH, N_BRANCH, BRANCH_W, d), BRANCH_W ** -0.5)
    inp["w_out"] = nrm((DEPTH, d, d), d ** -0.5)
    inp["router_w"] = nrm((DEPTH, d, N_EXPERTS), d ** -0.5)
    inp["router_bias"] = nrm((DEPTH, N_EXPERTS), 0.01)
    inp["exp_w1"] = nrm((DEPTH, N_EXPERTS, d, EXPERT_FF), d ** -0.5)
    inp["exp_w3"] = nrm((DEPTH, N_EXPERTS, d, EXPERT_FF), d ** -0.5)
    inp["exp_w2"] = nrm((DEPTH, N_EXPERTS, EXPERT_FF, d), EXPERT_FF ** -0.5)
    inp["sh_w1"] = nrm((DEPTH, d, SHARED_FF), d ** -0.5)
    inp["sh_w3"] = nrm((DEPTH, d, SHARED_FF), d ** -0.5)
    inp["sh_w2"] = nrm((DEPTH, SHARED_FF, d), SHARED_FF ** -0.5)
    return inp


def reference(x_prompt, x_sample, cache_k, cache_v, state_delta, state_ssd, c, c_ctx,
              ada_w, ada_b, norm_w, final_norm_w, w_in,
              dn_conv_w, dn_conv_b, dn_A_log, dn_dt_bias, dn_norm_w,
              ssd_conv_w, ssd_conv_b, ssd_A_log, ssd_dt_bias, ssd_D, ssd_norm_w,
              hy_conv_w, hy_conv_b, hy_f1_w, hy_f1_b, hy_f2_w, hy_f2_b, hy_f3_w, hy_freq, hy_decay, hy_bias,
              na_rpb, w_branch, w_out,
              router_w, router_bias, exp_w1, exp_w3, exp_w2, sh_w1, sh_w3, sh_w2):
    prm = dict(ada_w=ada_w, ada_b=ada_b, norm_w=norm_w, w_in=w_in,
               dn_conv_w=dn_conv_w, dn_conv_b=dn_conv_b, dn_A_log=dn_A_log, dn_dt_bias=dn_dt_bias, dn_norm_w=dn_norm_w,
               ssd_conv_w=ssd_conv_w, ssd_conv_b=ssd_conv_b, ssd_A_log=ssd_A_log, ssd_dt_bias=ssd_dt_bias,
               ssd_D=ssd_D, ssd_norm_w=ssd_norm_w,
               hy_conv_w=hy_conv_w, hy_conv_b=hy_conv_b, hy_f1_w=hy_f1_w, hy_f1_b=hy_f1_b, hy_f2_w=hy_f2_w,
               hy_f2_b=hy_f2_b, hy_f3_w=hy_f3_w, hy_freq=hy_freq, hy_decay=hy_decay, hy_bias=hy_bias,
               na_rpb=na_rpb, w_branch=w_branch, w_out=w_out,
               router_w=router_w, router_bias=router_bias, exp_w1=exp_w1, exp_w3=exp_w3, exp_w2=exp_w2,
               sh_w1=sh_w1, sh_w3=sh_w3, sh_w2=sh_w2)
    bp = x_prompt.shape[0]
    dn_zero = jnp.zeros((bp, 2, DN_HEADS, DN_DK, DN_DV), jnp.float32)
    ssd_zero = jnp.zeros((bp, 2, SSD_HEADS, SSD_P, SSD_N), jnp.float32)
    xp = x_prompt
    new_k, new_v, new_dn, new_ssd = [], [], [], []
    for l in range(DEPTH):
        xp, (k_l, v_l, dn_l, ssd_l) = trunk_layer(xp, c_ctx[None, :], l, prm, None, None, dn_zero, ssd_zero, True)
        new_k.append(k_l)
        new_v.append(v_l)
        new_dn.append(dn_l)
        new_ssd.append(ssd_l)
    y_prompt = rmsnorm(xp, final_norm_w)
    xs = x_sample
    for l in range(DEPTH):
        xs, _ = trunk_layer(xs, c, l, prm, cache_k[:, l], cache_v[:, l], state_delta[:, l], state_ssd[:, l], False)
    y_sample = rmsnorm(xs, final_norm_w)
    return (y_prompt, y_sample, jnp.stack(new_k, axis=1), jnp.stack(new_v, axis=1),
            jnp.stack(new_dn, axis=1), jnp.stack(new_ssd, axis=1))
```

```python
import functools
import math

import jax
import jax.numpy as jnp
import numpy as np
from jax import lax
from jax.experimental import pallas as pl
from jax.experimental.pallas import tpu as pltpu

BF = jnp.bfloat16
F32 = jnp.float32

D_MODEL = 2048
DEPTH = 2
DEC_SEQ = 4096
GRID_W = 64
EPS = 1e-6
NEG_INF = -1e30
N_BRANCH = 4
DN_HEADS = 4
DN_DK = 128
DN_DV = 128
CHUNK = 64
SSD_HEADS = 8
SSD_P = 64
SSD_GROUPS = 2
SSD_N = 128
HY_WIDTH = 512
HY_BANDS = 16
NA_HEADS = 8
NA_HD = 64
NA_WIN_R = 8
NA_WIN_C = 16
BRANCH_W = 512
N_EXPERTS = 64
TOP_K = 6
N_EXP_GROUPS = 8
TOPK_GROUPS = 4
EXPERT_FF = 512
ROUTED_SCALE = 2.5

DN_QKV = 2 * DN_HEADS * DN_DK + DN_HEADS * DN_DV
SSD_XBC = SSD_HEADS * SSD_P + 2 * SSD_GROUPS * SSD_N
IN_SPLITS = (DN_QKV, DN_HEADS * DN_DV, 2 * DN_HEADS, 2 * DN_HEADS,
             SSD_HEADS * SSD_P, SSD_XBC, 2 * SSD_HEADS,
             3 * HY_WIDTH, 3 * NA_HEADS * NA_HD, N_BRANCH * D_MODEL)

VMEM_LIMIT_BYTES = 56 * 1024 * 1024
LANES = 128
COL_DN_B = 0
COL_DN_A = 2 * DN_HEADS
COL_SSD_DT = 4 * DN_HEADS
CHUNK_GROUP = 4
MOE_BLOCK_ROWS = 512


def _params(sem):
    return pltpu.CompilerParams(dimension_semantics=sem, vmem_limit_bytes=VMEM_LIMIT_BYTES)


def _sigmoid(x):
    return jax.nn.sigmoid(x)


def _silu(x):
    return x * jax.nn.sigmoid(x)


def _dot(a, b):
    return jnp.dot(a.astype(BF), b.astype(BF), preferred_element_type=F32)


def _dot_nt(a, b):
    return lax.dot_general(a.astype(BF), b.astype(BF), (((1,), (1,)), ((), ())), preferred_element_type=F32)


def _dot_tn(a, b):
    return lax.dot_general(a.astype(BF), b.astype(BF), (((0,), (0,)), ((), ())), preferred_element_type=F32)


def _dot_f32(a, b):
    return jnp.dot(a, b, preferred_element_type=F32, precision=lax.Precision.HIGHEST)


def _mod_kernel(c_ref, w_ref, b_ref, o_ref):
    a = _silu(c_ref[...])
    o_ref[...] = _dot(a, w_ref[...]) + b_ref[...]


def ada_modulation(cvecs, ada_w, ada_b):
    n = ada_w.shape[-1]
    tn = 1024
    return pl.pallas_call(
        _mod_kernel,
        out_shape=jax.ShapeDtypeStruct((DEPTH, 8, n), F32),
        grid=(DEPTH, n // tn),
        in_specs=[pl.BlockSpec((8, D_MODEL), lambda l, j: (0, 0)),
                  pl.BlockSpec((None, D_MODEL, tn), lambda l, j: (l, 0, j)),
                  pl.BlockSpec((None, 1, tn), lambda l, j: (l, 0, j))],
        out_specs=pl.BlockSpec((None, 8, tn), lambda l, j: (l, 0, j)),
        compiler_params=_params(("parallel", "parallel")),
        name="ada_modulation",
    )(cvecs, ada_w, ada_b.reshape(DEPTH, 1, n))


def _mod_row(i, tm, n_prompt):
    start = i * tm
    return jnp.where(start < n_prompt, 0, 1 + (start - n_prompt) // DEC_SEQ)


def _norm_mod_kernel(x_ref, w_ref, sc_ref, sh_ref, o_ref):
    x = x_ref[...]
    y = x * lax.rsqrt(jnp.mean(x * x, axis=-1, keepdims=True) + EPS)
    o_ref[...] = ((y * w_ref[...]) * (1.0 + sc_ref[...]) + sh_ref[...]).astype(o_ref.dtype)


def _norm_kernel(x_ref, w_ref, o_ref):
    x = x_ref[...]
    y = x * lax.rsqrt(jnp.mean(x * x, axis=-1, keepdims=True) + EPS)
    o_ref[...] = (y * w_ref[...]).astype(o_ref.dtype)


def norm_modulate(x, w, scale, shift, n_prompt, out_dtype):
    t = x.shape[0]
    tm = 512
    row = functools.partial(_mod_row, tm=tm, n_prompt=n_prompt)
    return pl.pallas_call(
        _norm_mod_kernel,
        out_shape=jax.ShapeDtypeStruct((t, D_MODEL), out_dtype),
        grid=(t // tm,),
        in_specs=[pl.BlockSpec((tm, D_MODEL), lambda i: (i, 0)),
                  pl.BlockSpec((1, D_MODEL), lambda i: (0, 0)),
                  pl.BlockSpec((None, 1, D_MODEL), lambda i: (row(i), 0, 0)),
                  pl.BlockSpec((None, 1, D_MODEL), lambda i: (row(i), 0, 0))],
        out_specs=pl.BlockSpec((tm, D_MODEL), lambda i: (i, 0)),
        compiler_params=_params(("parallel",)),
        name="norm_modulate",
    )(x, w.reshape(1, D_MODEL), scale, shift)


def final_norm(x, w):
    t = x.shape[0]
    tm = 512
    return pl.pallas_call(
        _norm_kernel,
        out_shape=jax.ShapeDtypeStruct((t, D_MODEL), F32),
        grid=(t // tm,),
        in_specs=[pl.BlockSpec((tm, D_MODEL), lambda i: (i, 0)),
                  pl.BlockSpec((1, D_MODEL), lambda i: (0, 0))],
        out_specs=pl.BlockSpec((tm, D_MODEL), lambda i: (i, 0)),
        compiler_params=_params(("parallel",)),
        name="final_norm",
    )(x, w.reshape(1, D_MODEL))


def _mm_kernel(a_ref, w_ref, o_ref):
    o_ref[...] = jnp.dot(a_ref[...], w_ref[...], preferred_element_type=F32).astype(o_ref.dtype)


def _mm_residual_kernel(a_ref, w_ref, x_ref, g_ref, o_ref):
    acc = jnp.dot(a_ref[...], w_ref[...], preferred_element_type=F32)
    o_ref[...] = x_ref[...] + g_ref[...] * acc


def matmul(a, w, out_dtype, tm=1024, tn=512):
    m, k = a.shape
    n = w.shape[1]
    tn = min(tn, n)
    return pl.pallas_call(
        _mm_kernel,
        out_shape=jax.ShapeDtypeStruct((m, n), out_dtype),
        grid=(m // tm, n // tn),
        in_specs=[pl.BlockSpec((tm, k), lambda i, j: (i, 0)),
                  pl.BlockSpec((k, tn), lambda i, j: (0, j))],
        out_specs=pl.BlockSpec((tm, tn), lambda i, j: (i, j)),
        compiler_params=_params(("parallel", "arbitrary")),
        name="matmul",
    )(a, w)


def matmul_residual(a, w, x, gate, n_prompt, tm=1024, tn=512):
    m, k = a.shape
    n = w.shape[1]
    row = functools.partial(_mod_row, tm=tm, n_prompt=n_prompt)
    return pl.pallas_call(
        _mm_residual_kernel,
        out_shape=jax.ShapeDtypeStruct((m, n), F32),
        grid=(m // tm, n // tn),
        in_specs=[pl.BlockSpec((tm, k), lambda i, j: (i, 0)),
                  pl.BlockSpec((k, tn), lambda i, j: (0, j)),
                  pl.BlockSpec((tm, tn), lambda i, j: (i, j)),
                  pl.BlockSpec((None, 1, tn), lambda i, j: (row(i), 0, j))],
        out_specs=pl.BlockSpec((tm, tn), lambda i, j: (i, j)),
        compiler_params=_params(("parallel", "arbitrary")),
        name="matmul_residual",
    )(a, w, x, gate)


def _conv3(x, w, b):
    length = x.shape[0]
    row = lax.broadcasted_iota(jnp.int32, x.shape, 0)
    prev = jnp.where(row == 0, 0.0, pltpu.roll(x, 1, 0))
    nxt = jnp.where(row == length - 1, 0.0, pltpu.roll(x, length - 1, 0))
    return prev * w[0:1] + x * w[1:2] + nxt * w[2:3] + b


def _conv_silu_kernel(u_ref, w_ref, b_ref, o_ref):
    y = _conv3(u_ref[...].astype(F32), w_ref[...], b_ref[...])
    o_ref[...] = _silu(y).astype(o_ref.dtype)


def conv_silu(proj, conv_w, conv_b, width, seq, row0, batch):
    ct = 128 if seq > 1024 else 512
    rb0 = row0 // seq
    return pl.pallas_call(
        _conv_silu_kernel,
        out_shape=jax.ShapeDtypeStruct((batch * seq, width), BF),
        grid=(batch, width // ct),
        in_specs=[pl.BlockSpec((seq, ct), lambda b, j: (rb0 + b, j)),
                  pl.BlockSpec((3, ct), lambda b, j: (0, j)),
                  pl.BlockSpec((1, ct), lambda b, j: (0, j))],
        out_specs=pl.BlockSpec((seq, ct), lambda b, j: (b, j)),
        compiler_params=_params(("parallel", "parallel")),
        name="conv_silu",
    )(proj, conv_w, conv_b.reshape(1, width))


def _chunk_masks():
    ii = lax.broadcasted_iota(jnp.int32, (CHUNK, CHUNK), 0)
    jj = lax.broadcasted_iota(jnp.int32, (CHUNK, CHUNK), 1)
    return ii, jj


def _log_decay(p, alog, dtb):
    return -jnp.exp(alog) * jax.nn.softplus(p + dtb)


def _cumulative(gcol, grow, lm, um, reverse):
    if not reverse:
        return _dot_f32(lm, gcol), _dot_f32(grow, um)
    return _dot_f32(um, gcol), _dot_f32(grow, lm)


def _inv_unit_triangular(a, eye):
    p = eye - a
    ak = a
    for _ in range(5):
        ak = _dot_f32(ak, ak)
        p = p + _dot_f32(p, ak)
    return p


def _dn_scan_kernel(*refs, zero_init):
    if zero_init:
        (qf, kf, vf, pcf, prf, qb, kb_, vb, pcb, prb, pc_par, pr_par, of, ob, sref) = refs
        s0 = None
    else:
        (qf, kf, vf, pcf, prf, qb, kb_, vb, pcb, prb, pc_par, pr_par, s0, of, ob, sref) = refs
    g = pl.program_id(1)

    @pl.when(g == 0)
    def _():
        if zero_init:
            sref[...] = jnp.zeros(sref.shape, F32)
        else:
            sref[...] = s0[...]

    ii, jj = _chunk_masks()
    eye = (ii == jj).astype(F32)
    lm = (ii >= jj).astype(F32)
    um = (ii <= jj).astype(F32)
    alog_c = pc_par[0:1, :]
    dtb_c = pc_par[1:2, :]
    alog_r = pr_par[:, 0:1]
    dtb_r = pr_par[:, 1:2]
    dirs = ((qf, kf, vf, pcf, prf, of), (qb, kb_, vb, pcb, prb, ob))

    def chunk_step(c, carry):
        for d in range(2):
            q_ref, k_ref, v_ref, pc_ref, pr_ref, o_ref = dirs[d]
            cc = c if d == 0 else CHUNK_GROUP - 1 - c
            r0 = pl.multiple_of(cc * CHUNK, CHUNK)
            pc = pc_ref[pl.ds(r0, CHUNK), :]
            pr = pr_ref[cc]
            gc_col, gc_row = _cumulative(_log_decay(pc, alog_c, dtb_c), _log_decay(pr, alog_r, dtb_r),
                                         lm, um, reverse=(d == 1))
            beta_all = _sigmoid(pc)
            incl = (ii >= jj) if d == 0 else (ii <= jj)
            strict = (ii > jj) if d == 0 else (ii < jj)
            last = CHUNK - 1 if d == 0 else 0
            for h in range(DN_HEADS):
                ci = d * DN_HEADS + h
                sl = slice(h * DN_DK, (h + 1) * DN_DK)
                q = q_ref[pl.ds(r0, CHUNK), sl].astype(F32)
                k = k_ref[pl.ds(r0, CHUNK), sl].astype(F32)
                v = v_ref[pl.ds(r0, CHUNK), sl].astype(F32)
                q = q * lax.rsqrt(jnp.sum(q * q, axis=-1, keepdims=True) + EPS) * (DN_DK ** -0.5)
                k = k * lax.rsqrt(jnp.sum(k * k, axis=-1, keepdims=True) + EPS)
                gcc = gc_col[:, COL_DN_A + ci:COL_DN_A + ci + 1]
                gcr = gc_row[COL_DN_A + ci:COL_DN_A + ci + 1, :]
                glast = gc_col[last:last + 1, COL_DN_A + ci:COL_DN_A + ci + 1]
                decay = jnp.exp(jnp.where(incl, gcc - gcr, -jnp.inf))
                beta = beta_all[:, COL_DN_B + ci:COL_DN_B + ci + 1]
                kbeta = k * beta
                a_mat = jnp.where(strict, _dot_nt(kbeta, k) * decay, 0.0)
                t_mat = _inv_unit_triangular(a_mat, eye)
                egc = jnp.exp(gcc)
                sol = _dot_f32(t_mat, jnp.concatenate([v * beta, kbeta * egc], axis=-1))
                u = sol[:, :DN_DV]
                w = sol[:, DN_DV:]
                s = sref[d, h]
                v_new = u - _dot(w, s)
                attn = _dot_nt(q, k) * decay
                o_ref[pl.ds(r0, CHUNK), sl] = _dot(q * egc, s) + _dot(attn, v_new)
                sref[d, h] = s * jnp.exp(glast) + _dot_tn(k * jnp.exp(glast - gcc), v_new)
        return carry

    lax.fori_loop(0, CHUNK_GROUP, chunk_step, 0)


def deltanet_scan(qkv, pcol, prow, pc_par, pr_par, s0, layer, seq, row0, batch):
    gt = CHUNK_GROUP * CHUNK
    ncg = seq // gt
    hw = DN_HEADS * DN_DK
    pc0 = row0 // gt
    zero_init = s0 is None

    def fwd(b, g):
        return b * ncg + g

    def bwd(b, g):
        return b * ncg + ncg - 1 - g

    def qkv_specs(pos):
        return [pl.BlockSpec((gt, hw), lambda b, g, n=n: (pos(b, g), n)) for n in range(3)]

    def gate_specs(pos):
        return [pl.BlockSpec((gt, LANES), lambda b, g: (pc0 + pos(b, g), 0)),
                pl.BlockSpec((CHUNK_GROUP, LANES, CHUNK), lambda b, g: (pc0 + pos(b, g), 0, 0))]

    in_specs = (qkv_specs(fwd) + gate_specs(fwd) + qkv_specs(bwd) + gate_specs(bwd)
                + [pl.BlockSpec((8, LANES), lambda b, g: (0, 0)),
                   pl.BlockSpec((LANES, 8), lambda b, g: (0, 0))])
    args = [qkv, qkv, qkv, pcol, prow, qkv, qkv, qkv, pcol, prow, pc_par, pr_par]
    if not zero_init:
        in_specs.append(pl.BlockSpec((None, None, 2, DN_HEADS, DN_DK, DN_DV),
                                     lambda b, g: (b, layer, 0, 0, 0, 0)))
        args.append(s0)
    return pl.pallas_call(
        functools.partial(_dn_scan_kernel, zero_init=zero_init),
        out_shape=(jax.ShapeDtypeStruct((batch * seq, hw), F32),
                   jax.ShapeDtypeStruct((batch * seq, hw), F32),
                   jax.ShapeDtypeStruct((batch, 2, DN_HEADS, DN_DK, DN_DV), F32)),
        grid=(batch, ncg),
        in_specs=in_specs,
        out_specs=(pl.BlockSpec((gt, hw), lambda b, g: (fwd(b, g), 0)),
                   pl.BlockSpec((gt, hw), lambda b, g: (bwd(b, g), 0)),
                   pl.BlockSpec((None, 2, DN_HEADS, DN_DK, DN_DV), lambda b, g: (b, 0, 0, 0, 0))),
        compiler_params=_params(("parallel", "arbitrary")),
        name="deltanet_scan",
    )(*args)


def _dn_out_kernel(of_ref, ob_ref, z_ref, w_ref, o_ref):
    for h in range(DN_HEADS):
        sl = slice(h * DN_DV, (h + 1) * DN_DV)
        s = of_ref[:, sl] + ob_ref[:, sl]
        y = s * lax.rsqrt(jnp.mean(s * s, axis=-1, keepdims=True) + EPS) * w_ref[...]
        o_ref[:, sl] = (y * _silu(z_ref[:, sl].astype(F32))).astype(o_ref.dtype)


def deltanet_out(o_f, o_b, proj, norm_w, row0):
    n = o_f.shape[0]
    tm = 512
    hw = DN_HEADS * DN_DV
    return pl.pallas_call(
        _dn_out_kernel,
        out_shape=jax.ShapeDtypeStruct((n, hw), BF),
        grid=(n // tm,),
        in_specs=[pl.BlockSpec((tm, hw), lambda i: (i, 0)),
                  pl.BlockSpec((tm, hw), lambda i: (i, 0)),
                  pl.BlockSpec((tm, hw), lambda i: (row0 // tm + i, DN_QKV // hw)),
                  pl.BlockSpec((1, DN_DV), lambda i: (0, 0))],
        out_specs=pl.BlockSpec((tm, hw), lambda i: (i, 0)),
        compiler_params=_params(("parallel",)),
        name="deltanet_out",
    )(o_f, o_b, proj, norm_w.reshape(1, DN_DV))


def gate_params(dn_a_log, dn_dt_bias, ssd_a_log, ssd_dt_bias):
    def lay(dn, ssd):
        v = jnp.zeros((LANES,), F32)
        v = v.at[COL_DN_A:COL_DN_A + 2 * DN_HEADS].set(dn.reshape(-1).astype(F32))
        return v.at[COL_SSD_DT:COL_SSD_DT + 2 * SSD_HEADS].set(ssd.reshape(-1).astype(F32))
    par = jnp.zeros((8, LANES), F32).at[0].set(lay(dn_a_log, ssd_a_log)).at[1].set(lay(dn_dt_bias, ssd_dt_bias))
    return par, par.T


SSD_XW = SSD_HEADS * SSD_P
SSD_GW = SSD_GROUPS * SSD_N
HEADS_PER_GROUP = SSD_HEADS // SSD_GROUPS


def _ssd_scan_kernel(*refs, zero_init):
    if zero_init:
        (xf, pcf, prf, xb, pcb, prb, pc_par, pr_par, of, ob, sref) = refs
        s0 = None
    else:
        (xf, pcf, prf, xb, pcb, prb, pc_par, pr_par, s0, of, ob, sref) = refs
    g = pl.program_id(1)

    @pl.when(g == 0)
    def _():
        if zero_init:
            sref[...] = jnp.zeros(sref.shape, F32)
        else:
            sref[...] = s0[...]

    ii, jj = _chunk_masks()
    lm = (ii >= jj).astype(F32)
    um = (ii <= jj).astype(F32)
    alog_c = pc_par[0:1, :]
    dtb_c = pc_par[1:2, :]
    alog_r = pr_par[:, 0:1]
    dtb_r = pr_par[:, 1:2]
    dirs = ((xf, pcf, prf, of), (xb, pcb, prb, ob))
    gw = HEADS_PER_GROUP * SSD_P

    def chunk_step(c, carry):
        for d in range(2):
            x_ref, pc_ref, pr_ref, o_ref = dirs[d]
            cc = c if d == 0 else CHUNK_GROUP - 1 - c
            r0 = pl.multiple_of(cc * CHUNK, CHUNK)
            rows = pl.ds(r0, CHUNK)
            pc = pc_ref[rows, :]
            pr = pr_ref[cc]
            dt_all = jax.nn.softplus(pc + dtb_c)
            ac_col, ac_row = _cumulative(_log_decay(pc, alog_c, dtb_c), _log_decay(pr, alog_r, dtb_r),
                                         lm, um, reverse=(d == 1))
            incl = (ii >= jj) if d == 0 else (ii <= jj)
            last = CHUNK - 1 if d == 0 else 0
            for gi in range(SSD_GROUPS):
                bm = x_ref[rows, SSD_XW + gi * SSD_N:SSD_XW + (gi + 1) * SSD_N]
                cm = x_ref[rows, SSD_XW + SSD_GW + gi * SSD_N:SSD_XW + SSD_GW + (gi + 1) * SSD_N]
                scores = _dot_nt(cm, bm)
                srow = pl.ds(gi * gw, gw)
                s_grp = sref[d, srow, :]
                y_state = _dot_nt(cm, s_grp)
                ys, xs, decs = [], [], []
                for hh in range(HEADS_PER_GROUP):
                    h = gi * HEADS_PER_GROUP + hh
                    col = COL_SSD_DT + d * SSD_HEADS + h
                    acc = ac_col[:, col:col + 1]
                    acr = ac_row[col:col + 1, :]
                    alast = ac_col[last:last + 1, col:col + 1]
                    decay = jnp.exp(jnp.where(incl, acc - acr, -jnp.inf))
                    xdt = x_ref[rows, h * SSD_P:(h + 1) * SSD_P].astype(F32) * dt_all[:, col:col + 1]
                    ys.append(_dot(scores * decay, xdt) + y_state[:, hh * SSD_P:(hh + 1) * SSD_P] * jnp.exp(acc))
                    xs.append(xdt * jnp.exp(alast - acc))
                    decs.append(jnp.broadcast_to(jnp.exp(alast), (SSD_P, 1)))
                o_ref[rows, gi * gw:(gi + 1) * gw] = jnp.concatenate(ys, axis=-1)
                s_add = _dot_tn(jnp.concatenate(xs, axis=-1), bm)
                sref[d, srow, :] = s_grp * jnp.concatenate(decs, axis=0) + s_add
        return carry

    lax.fori_loop(0, CHUNK_GROUP, chunk_step, 0)


def ssd_scan(xbc, pcol, prow, pc_par, pr_par, s0, layer, seq, row0, batch):
    gt = CHUNK_GROUP * CHUNK
    ncg = seq // gt
    pc0 = row0 // gt
    zero_init = s0 is None

    def fwd(b, g):
        return b * ncg + g

    def bwd(b, g):
        return b * ncg + ncg - 1 - g

    def specs(pos):
        return [pl.BlockSpec((gt, SSD_XBC), lambda b, g: (pos(b, g), 0)),
                pl.BlockSpec((gt, LANES), lambda b, g: (pc0 + pos(b, g), 0)),
                pl.BlockSpec((CHUNK_GROUP, LANES, CHUNK), lambda b, g: (pc0 + pos(b, g), 0, 0))]

    in_specs = specs(fwd) + specs(bwd) + [pl.BlockSpec((8, LANES), lambda b, g: (0, 0)),
                                          pl.BlockSpec((LANES, 8), lambda b, g: (0, 0))]
    args = [xbc, pcol, prow, xbc, pcol, prow, pc_par, pr_par]
    if not zero_init:
        in_specs.append(pl.BlockSpec((None, None, 2, SSD_XW, SSD_N), lambda b, g: (b, layer, 0, 0, 0)))
        args.append(s0)
    return pl.pallas_call(
        functools.partial(_ssd_scan_kernel, zero_init=zero_init),
        out_shape=(jax.ShapeDtypeStruct((batch * seq, SSD_XW), F32),
                   jax.ShapeDtypeStruct((batch * seq, SSD_XW), F32),
                   jax.ShapeDtypeStruct((batch, 2, SSD_XW, SSD_N), F32)),
        grid=(batch, ncg),
        in_specs=in_specs,
        out_specs=(pl.BlockSpec((gt, SSD_XW), lambda b, g: (fwd(b, g), 0)),
                   pl.BlockSpec((gt, SSD_XW), lambda b, g: (bwd(b, g), 0)),
                   pl.BlockSpec((None, 2, SSD_XW, SSD_N), lambda b, g: (b, 0, 0, 0))),
        compiler_params=_params(("parallel", "arbitrary")),
        name="ssd_scan",
    )(*args)


def _ssd_out_kernel(yf_ref, yb_ref, x_ref, z_ref, d_ref, w_ref, o_ref):
    y = yf_ref[...] + yb_ref[...] + x_ref[...].astype(F32) * d_ref[...]
    y = y * _silu(z_ref[...].astype(F32))
    o_ref[...] = (y * lax.rsqrt(jnp.mean(y * y, axis=-1, keepdims=True) + EPS) * w_ref[...]).astype(o_ref.dtype)


def ssd_out(y_f, y_b, xbc, proj, d_skip, norm_w, row0):
    n = y_f.shape[0]
    tm = 512
    return pl.pallas_call(
        _ssd_out_kernel,
        out_shape=jax.ShapeDtypeStruct((n, SSD_XW), BF),
        grid=(n // tm,),
        in_specs=[pl.BlockSpec((tm, SSD_XW), lambda i: (i, 0)),
                  pl.BlockSpec((tm, SSD_XW), lambda i: (i, 0)),
                  pl.BlockSpec((tm, SSD_XW), lambda i: (i, 0)),
                  pl.BlockSpec((tm, SSD_XW), lambda i: (row0 // tm + i, SSD_XBC // SSD_XW)),
                  pl.BlockSpec((1, SSD_XW), lambda i: (0, 0)),
                  pl.BlockSpec((1, SSD_XW), lambda i: (0, 0))],
        out_specs=pl.BlockSpec((tm, SSD_XW), lambda i: (i, 0)),
        compiler_params=_params(("parallel",)),
        name="ssd_out",
    )(y_f, y_b, xbc, proj, jnp.repeat(d_skip.astype(F32), SSD_P).reshape(1, SSD_XW), norm_w.reshape(1, SSD_XW))


HY_FEAT_PAD = 128
HY_HID = 64


def _hy_filter_kernel(feat_ref, f1w, f1b, f2w, f2b, f3w, fq, dec, o_ref):
    feat = feat_ref[...]
    hdn = jnp.sin(fq[...] * (_dot_f32(feat, f1w[...]) + f1b[...]))
    hdn = jnp.sin(fq[...] * (_dot_f32(hdn, f2w[...]) + f2b[...]))
    filt = _dot_f32(hdn, f3w[...])
    window = jnp.exp(-jnp.abs(dec[...]) * (2.0 * jnp.abs(feat[:, 0:1])))
    o_ref[...] = filt * window


def hyena_filter(seq, f1_w, f1_b, f2_w, f2_b, f3_w, freq, decay):
    off = (jnp.arange(seq, dtype=F32) - (seq // 2)) / seq
    ang = 2.0 * math.pi * off[:, None] * jnp.arange(1, HY_BANDS + 1, dtype=F32)
    feat = jnp.concatenate([off[:, None], jnp.sin(ang), jnp.cos(ang)], axis=-1)
    emb = feat.shape[1]
    feat = jnp.pad(feat, ((0, 0), (0, HY_FEAT_PAD - emb)))
    f1p = jnp.pad(f1_w.astype(F32), ((0, HY_FEAT_PAD - emb), (0, 0)))
    tl = min(seq, 512)
    full = lambda shape: pl.BlockSpec(shape, lambda i: (0, 0))
    return pl.pallas_call(
        _hy_filter_kernel,
        out_shape=jax.ShapeDtypeStruct((seq, HY_WIDTH), F32),
        grid=(seq // tl,),
        in_specs=[pl.BlockSpec((tl, HY_FEAT_PAD), lambda i: (i, 0)),
                  full((HY_FEAT_PAD, HY_HID)), full((1, HY_HID)), full((HY_HID, HY_HID)), full((1, HY_HID)),
                  full((HY_HID, HY_WIDTH)), full((1, HY_HID)), full((1, HY_WIDTH))],
        out_specs=pl.BlockSpec((tl, HY_WIDTH), lambda i: (i, 0)),
        compiler_params=_params(("parallel",)),
        name="hyena_filter",
    )(feat, f1p, f1_b.reshape(1, HY_HID), f2_w, f2_b.reshape(1, HY_HID), f3_w,
      freq.reshape(1, HY_HID), decay.reshape(1, HY_WIDTH))


def _hy_pre_kernel(u0, u1, u2, w0, w1, w2, b0, b1, b2, x0_ref, vv_ref):
    x0 = _conv3(u0[...].astype(F32), w0[...], b0[...])
    x1 = _conv3(u1[...].astype(F32), w1[...], b1[...])
    v = _conv3(u2[...].astype(F32), w2[...], b2[...])
    x0_ref[...] = x0.astype(x0_ref.dtype)
    vv_ref[...] = (v * x1).astype(vv_ref.dtype)


def hyena_pre(proj, conv_w, conv_b, seq, row0, batch):
    ct = 128
    nct = HY_WIDTH // ct
    rb0 = row0 // seq
    cb = conv_b.reshape(1, 3 * HY_WIDTH)
    u_specs = [pl.BlockSpec((seq, ct), lambda b, j, n=n: (rb0 + b, n * nct + j)) for n in range(3)]
    w_specs = [pl.BlockSpec((3, ct), lambda b, j, n=n: (0, n * nct + j)) for n in range(3)]
    b_specs = [pl.BlockSpec((1, ct), lambda b, j, n=n: (0, n * nct + j)) for n in range(3)]
    return pl.pallas_call(
        _hy_pre_kernel,
        out_shape=(jax.ShapeDtypeStruct((batch * seq, HY_WIDTH), BF),
                   jax.ShapeDtypeStruct((batch, seq, HY_WIDTH), BF)),
        grid=(batch, nct),
        in_specs=u_specs + w_specs + b_specs,
        out_specs=(pl.BlockSpec((seq, ct), lambda b, j: (b, j)),
                   pl.BlockSpec((None, seq, ct), lambda b, j: (b, 0, j))),
        compiler_params=_params(("parallel", "parallel")),
        name="hyena_pre",
    )(proj, proj, proj, conv_w, conv_w, conv_w, cb, cb, cb)


def _trig_tables(seq):
    n = 2 * seq
    nblk = seq // LANES

    def trig(m):
        ang = (m % (2 * n)).astype(F32) * (math.pi / n)
        return jnp.cos(ang), jnp.sin(ang)

    k = jnp.arange(seq, dtype=jnp.int32)[:, None]
    blk = jnp.arange(nblk, dtype=jnp.int32)[None, :]
    lane = jnp.arange(LANES, dtype=jnp.int32)[None, :]
    ca, sa = trig((2 * k + 1) * (LANES * blk))
    fwd_c0 = jnp.concatenate([ca, -sa], axis=0)
    fwd_s0 = jnp.concatenate([sa, ca], axis=0)
    fwd_cd, fwd_sd = trig((2 * k + 1) * lane)
    tp = k + seq // 2
    ca, sa = trig((2 * LANES * blk + 1) * tp)
    scale = 2.0 / n
    inv_c0 = jnp.concatenate([ca, -sa], axis=1) * scale
    inv_s0 = jnp.concatenate([sa, ca], axis=1) * scale
    inv_cd, inv_sd = trig(2 * lane * tp)
    return (fwd_c0, fwd_s0, fwd_cd, fwd_sd), (inv_c0, inv_s0, inv_cd, inv_sd)


def _trig_tile(dst, r0, rows, c0, s0, cd, sd):
    for j in range(c0.shape[1]):
        dst[r0:r0 + rows, j * LANES:(j + 1) * LANES] = (c0[:, j:j + 1] * cd - s0[:, j:j + 1] * sd).astype(dst.dtype)


def _dft_fwd_kernel(*refs, mul_h, tk):
    if mul_h:
        c0a, s0a, c0b, s0b, cd, sd, x_ref, h_ref, o_ref, lhs = refs
    else:
        c0a, s0a, c0b, s0b, cd, sd, x_ref, o_ref, lhs = refs

    @pl.when(pl.program_id(1) == 0)
    def _():
        _trig_tile(lhs, 0, tk, c0a[...], s0a[...], cd[...], sd[...])
        _trig_tile(lhs, tk, tk, c0b[...], s0b[...], cd[...], sd[...])

    acc = jnp.dot(lhs[...], x_ref[...].astype(BF), preferred_element_type=F32)
    ure = acc[:tk]
    uim = acc[tk:]
    if mul_h:
        hre = h_ref[0]
        him = h_ref[1]
        o_ref[0] = (ure * hre - uim * him).astype(o_ref.dtype)
        o_ref[1] = (ure * him + uim * hre).astype(o_ref.dtype)
    else:
        o_ref[0] = ure.astype(o_ref.dtype)
        o_ref[1] = uim.astype(o_ref.dtype)


def dft_forward(x, tables, h, out_dtype):
    batch, seq, width = x.shape
    c0, s0, cd, sd = tables
    nblk = seq // LANES
    tk = min(256, seq)
    nk = seq // tk
    mul_h = h is not None
    in_specs = [pl.BlockSpec((tk, nblk), lambda i, b: (i, 0)),
                pl.BlockSpec((tk, nblk), lambda i, b: (i, 0)),
                pl.BlockSpec((tk, nblk), lambda i, b: (nk + i, 0)),
                pl.BlockSpec((tk, nblk), lambda i, b: (nk + i, 0)),
                pl.BlockSpec((tk, LANES), lambda i, b: (i, 0)),
                pl.BlockSpec((tk, LANES), lambda i, b: (i, 0)),
                pl.BlockSpec((None, seq, width), lambda i, b: (b, 0, 0))]
    args = [c0, s0, c0, s0, cd, sd, x]
    if mul_h:
        in_specs.append(pl.BlockSpec((None, 2, tk, width), lambda i, b: (0, 0, i, 0)))
        args.append(h)
    return pl.pallas_call(
        functools.partial(_dft_fwd_kernel, mul_h=mul_h, tk=tk),
        out_shape=jax.ShapeDtypeStruct((batch, 2, seq, width), out_dtype),
        grid=(nk, batch),
        in_specs=in_specs,
        out_specs=pl.BlockSpec((None, 2, tk, width), lambda i, b: (b, 0, i, 0)),
        scratch_shapes=[pltpu.VMEM((2 * tk, seq), BF)],
        compiler_params=_params(("parallel", "arbitrary")),
        name="dft_forward",
    )(*args)


def _dft_inv_kernel(c0, s0, cd, sd, y_ref, vv_ref, x0_ref, bias_ref, o_ref, lhs):
    @pl.when(pl.program_id(1) == 0)
    def _():
        _trig_tile(lhs, 0, lhs.shape[0], c0[...], s0[...], cd[...], sd[...])

    conv = jnp.dot(lhs[...], y_ref[...], preferred_element_type=F32)
    vv = vv_ref[...].astype(F32)
    o_ref[...] = ((conv + vv * bias_ref[...]) * x0_ref[...].astype(F32)).astype(o_ref.dtype)


def dft_inverse_gate(y, tables, vv, x0, bias):
    batch, seq, width = vv.shape
    c0, s0, cd, sd = tables
    nblk2 = c0.shape[1]
    tm = min(256, seq)
    nt = seq // tm
    return pl.pallas_call(
        _dft_inv_kernel,
        out_shape=jax.ShapeDtypeStruct((batch * seq, width), BF),
        grid=(nt, batch),
        in_specs=[pl.BlockSpec((tm, nblk2), lambda i, b: (i, 0)),
                  pl.BlockSpec((tm, nblk2), lambda i, b: (i, 0)),
                  pl.BlockSpec((tm, LANES), lambda i, b: (i, 0)),
                  pl.BlockSpec((tm, LANES), lambda i, b: (i, 0)),
                  pl.BlockSpec((None, 2 * seq, width), lambda i, b: (b, 0, 0)),
                  pl.BlockSpec((None, tm, width), lambda i, b: (b, i, 0)),
                  pl.BlockSpec((tm, width), lambda i, b: (b * nt + i, 0)),
                  pl.BlockSpec((1, width), lambda i, b: (0, 0))],
        out_specs=pl.BlockSpec((tm, width), lambda i, b: (b * nt + i, 0)),
        scratch_shapes=[pltpu.VMEM((tm, 2 * seq), BF)],
        compiler_params=_params(("parallel", "arbitrary")),
        name="dft_inverse_gate",
    )(c0, s0, cd, sd, y, vv, x0, bias.reshape(1, width).astype(F32))


def hyena_mixer(proj, prm, layer, seq, row0, batch):
    filt = hyena_filter(seq, prm["hy_f1_w"][layer], prm["hy_f1_b"][layer], prm["hy_f2_w"][layer],
                        prm["hy_f2_b"][layer], prm["hy_f3_w"][layer], prm["hy_freq"][layer],
                        prm["hy_decay"][layer])
    fwd_tab, inv_tab = _trig_tables(seq)
    h_spec = dft_forward(filt[None], fwd_tab, None, F32)
    x0, vv = hyena_pre(proj, prm["hy_conv_w"][layer], prm["hy_conv_b"][layer], seq, row0, batch)
    y_spec = dft_forward(vv, fwd_tab, h_spec, BF)
    return dft_inverse_gate(y_spec.reshape(batch, 2 * seq, HY_WIDTH), inv_tab, vv, x0, prm["hy_bias"][layer])


NA_W = NA_HEADS * NA_HD
NA_SCALE = NA_HD ** -0.5


def _softmax_attend(scores, values):
    m = functools.reduce(jnp.maximum, [jnp.max(s, axis=-1, keepdims=True) for s in scores])
    ps = [jnp.exp(s - m) for s in scores]
    denom = functools.reduce(lambda a, b: a + b, [jnp.sum(p, axis=-1, keepdims=True) for p in ps])
    num = functools.reduce(lambda a, b: a + b, [_dot(p, v) for p, v in zip(ps, values)])
    return num / denom


def _ctx_attn_kernel(q_ref, k_ref, v_ref, o_ref):
    outs = []
    for h in range(NA_HEADS):
        sl = slice(h * NA_HD, (h + 1) * NA_HD)
        s = _dot_nt(q_ref[:, sl], k_ref[:, sl]) * NA_SCALE
        outs.append(_softmax_attend([s], [v_ref[:, sl]]))
    o_ref[...] = jnp.concatenate(outs, axis=-1).astype(o_ref.dtype)


def context_attention(proj, seq, batch):
    return pl.pallas_call(
        _ctx_attn_kernel,
        out_shape=jax.ShapeDtypeStruct((batch * seq, NA_W), BF),
        grid=(batch,),
        in_specs=[pl.BlockSpec((seq, NA_W), lambda b, n=n: (b, n)) for n in range(3)],
        out_specs=pl.BlockSpec((seq, NA_W), lambda b: (b, 0)),
        compiler_params=_params(("parallel",)),
        name="context_attention",
    )(proj, proj, proj)


def _na_row_start(r, rows_n):
    return jnp.clip(r - NA_WIN_R // 2, 0, rows_n - NA_WIN_R)


def _nbr_attn_kernel(q_ref, k_ref, v_ref, kc_ref, vc_ref, bias_ref, o_ref, *, rows_n):
    r = pl.program_id(1)
    koff = pl.multiple_of(_na_row_start(r, rows_n) * GRID_W, GRID_W)
    win = pl.ds(koff, NA_WIN_R * GRID_W)
    outs = []
    for h in range(NA_HEADS):
        sl = slice(h * NA_HD, (h + 1) * NA_HD)
        q = q_ref[:, sl]
        bias = bias_ref[h]
        s_loc = _dot_nt(q, k_ref[win, sl]) * NA_SCALE
        s_loc = jnp.where(bias > 0.5 * NEG_INF, s_loc + bias, NEG_INF)
        s_ctx = _dot_nt(q, kc_ref[:, sl]) * NA_SCALE
        outs.append(_softmax_attend([s_loc, s_ctx], [v_ref[win, sl], vc_ref[:, sl]]))
    o_ref[...] = jnp.concatenate(outs, axis=-1).astype(o_ref.dtype)


def neighbourhood_bias_table(rpb):
    cols = jnp.arange(GRID_W)
    col_start = jnp.clip(cols - NA_WIN_C // 2, 0, GRID_W - NA_WIN_C)
    col_in = (cols[None, :] >= col_start[:, None]) & (cols[None, :] < col_start[:, None] + NA_WIN_C)
    dc = jnp.clip(cols[None, :] - cols[:, None], -(NA_WIN_C - 1), NA_WIN_C - 1) + (NA_WIN_C - 1)
    var = jnp.arange(NA_WIN_R)
    dr = jnp.arange(NA_WIN_R)[None, :] - var[:, None] + (NA_WIN_R - 1)
    tab = rpb.astype(F32)[:, dr[:, None, :, None], dc[None, :, None, :]]
    tab = jnp.where(col_in[None, None, :, None, :], tab, NEG_INF)
    return tab.transpose(1, 0, 2, 3, 4).reshape(NA_WIN_R, NA_HEADS, GRID_W, NA_WIN_R * GRID_W)


def neighbourhood_attention(proj, cache_k, cache_v, bias_tab, layer, seq, row0, batch):
    rows_n = seq // GRID_W
    assert rows_n >= NA_WIN_R
    past = cache_k.shape[2]
    qb0 = row0 // GRID_W
    sb0 = row0 // seq

    def variant(r):
        return r - _na_row_start(r, rows_n)

    return pl.pallas_call(
        functools.partial(_nbr_attn_kernel, rows_n=rows_n),
        out_shape=jax.ShapeDtypeStruct((batch * seq, NA_W), BF),
        grid=(batch, rows_n),
        in_specs=[pl.BlockSpec((GRID_W, NA_W), lambda b, r: (qb0 + b * rows_n + r, 0)),
                  pl.BlockSpec((seq, NA_W), lambda b, r: (sb0 + b, 1)),
                  pl.BlockSpec((seq, NA_W), lambda b, r: (sb0 + b, 2)),
                  pl.BlockSpec((None, None, past, NA_W), lambda b, r: (b, layer, 0, 0)),
                  pl.BlockSpec((None, None, past, NA_W), lambda b, r: (b, layer, 0, 0)),
                  pl.BlockSpec((None, NA_HEADS, GRID_W, NA_WIN_R * GRID_W), lambda b, r: (variant(r), 0, 0, 0))],
        out_specs=pl.BlockSpec((GRID_W, NA_W), lambda b, r: (b * rows_n + r, 0)),
        compiler_params=_params(("parallel", "arbitrary")),
        name="neighbourhood_attention",
    )(proj, proj, proj, cache_k, cache_v, bias_tab)


def _merge_kernel(oa, ob, oc, od, g0, g1, g2, g3, wb, o_ref):
    acc = None
    for n, (o_n, g_n) in enumerate(((oa, g0), (ob, g1), (oc, g2), (od, g3))):
        term = _sigmoid(g_n[...].astype(F32)) * jnp.dot(o_n[...], wb[n], preferred_element_type=F32)
        acc = term if acc is None else acc + term
    o_ref[...] = acc.astype(o_ref.dtype)


def branch_merge(branches, gate_proj, w_branch):
    t = gate_proj.shape[0]
    tm, tn = 512, 1024
    nj = D_MODEL // tn
    return pl.pallas_call(
        _merge_kernel,
        out_shape=jax.ShapeDtypeStruct((t, D_MODEL), BF),
        grid=(t // tm, nj),
        in_specs=([pl.BlockSpec((tm, BRANCH_W), lambda i, j: (i, 0)) for _ in range(N_BRANCH)]
                  + [pl.BlockSpec((tm, tn), lambda i, j, n=n: (i, n * nj + j)) for n in range(N_BRANCH)]
                  + [pl.BlockSpec((N_BRANCH, BRANCH_W, tn), lambda i, j: (0, 0, j))]),
        out_specs=pl.BlockSpec((tm, tn), lambda i, j: (i, j)),
        compiler_params=_params(("parallel", "arbitrary")),
        name="branch_merge",
    )(*branches, gate_proj, gate_proj, gate_proj, gate_proj, w_branch)


EXP_PER_GROUP = N_EXPERTS // N_EXP_GROUPS


def _first_argmax(v, idx, size):
    m = jnp.max(v, axis=0, keepdims=True)
    first = jnp.min(jnp.where(v == m, idx, size), axis=0, keepdims=True)
    return m, first


def _router_kernel(h_ref, rw_ref, b_ref, e_ref, w_ref):
    logits = lax.dot_general(rw_ref[...], h_ref[...], (((1,), (1,)), ((), ())), preferred_element_type=F32,
                             precision=lax.Precision.HIGHEST)
    scores = _sigmoid(logits)
    biased = scores + b_ref[...]
    tm = scores.shape[1]
    sub = lax.broadcasted_iota(jnp.int32, (EXP_PER_GROUP, tm), 0)
    grp = []
    for g in range(N_EXP_GROUPS):
        v = biased[g * EXP_PER_GROUP:(g + 1) * EXP_PER_GROUP]
        m1, first = _first_argmax(v, sub, EXP_PER_GROUP)
        m2 = jnp.max(jnp.where(sub == first, -jnp.inf, v), axis=0, keepdims=True)
        grp.append(m1 + m2)
    cur = jnp.concatenate(grp, axis=0)
    gidx = lax.broadcasted_iota(jnp.int32, (N_EXP_GROUPS, tm), 0)
    sel = jnp.zeros((N_EXP_GROUPS, tm), F32)
    for _ in range(TOPK_GROUPS):
        _, first = _first_argmax(cur, gidx, N_EXP_GROUPS)
        pick = gidx == first
        sel = jnp.where(pick, 1.0, sel)
        cur = jnp.where(pick, -jnp.inf, cur)
    cur = jnp.concatenate(
        [jnp.where(sel[g:g + 1] > 0.5, biased[g * EXP_PER_GROUP:(g + 1) * EXP_PER_GROUP], NEG_INF)
         for g in range(N_EXP_GROUPS)], axis=0)
    eidx = lax.broadcasted_iota(jnp.int32, (N_EXPERTS, tm), 0)
    es, ws = [], []
    for _ in range(TOP_K):
        _, first = _first_argmax(cur, eidx, N_EXPERTS)
        pick = eidx == first
        es.append(first)
        ws.append(jnp.sum(jnp.where(pick, scores, 0.0), axis=0, keepdims=True))
        cur = jnp.where(pick, -jnp.inf, cur)
    total = functools.reduce(lambda a, b: a + b, ws)
    pad = 8 - TOP_K
    e_ref[...] = jnp.concatenate(es + [jnp.zeros((pad, tm), jnp.int32)], axis=0)
    w_ref[...] = jnp.concatenate([w / total * ROUTED_SCALE for w in ws] + [jnp.zeros((pad, tm), F32)], axis=0)


def moe_route(h, router_w, router_bias):
    t = h.shape[0]
    tm = 512
    return pl.pallas_call(
        _router_kernel,
        out_shape=(jax.ShapeDtypeStruct((8, t), jnp.int32), jax.ShapeDtypeStruct((8, t), F32)),
        grid=(t // tm,),
        in_specs=[pl.BlockSpec((tm, D_MODEL), lambda i: (i, 0)),
                  pl.BlockSpec((N_EXPERTS, D_MODEL), lambda i: (0, 0)),
                  pl.BlockSpec((N_EXPERTS, 1), lambda i: (0, 0))],
        out_specs=(pl.BlockSpec((8, tm), lambda i: (0, i)), pl.BlockSpec((8, tm), lambda i: (0, i))),
        compiler_params=_params(("parallel",)),
        name="moe_route",
    )(h, router_w.T.astype(F32), router_bias.reshape(N_EXPERTS, 1).astype(F32))


GATHER_ROWS = 1024


def _gather_kernel(idx_hbm, src, dst, idx_smem, sem_idx, sem_rows):
    i = pl.program_id(0)
    load = pltpu.make_async_copy(idx_hbm.at[i], idx_smem, sem_idx)
    load.start()
    load.wait()
    base = i * GATHER_ROWS

    def issue(r, carry):
        pltpu.make_async_copy(src.at[pl.ds(idx_smem[r], 1)], dst.at[pl.ds(base + r, 1)], sem_rows).start()
        return carry

    lax.fori_loop(0, GATHER_ROWS, issue, 0)
    pltpu.make_async_copy(src.at[pl.ds(0, GATHER_ROWS)], dst.at[pl.ds(base, GATHER_ROWS)], sem_rows).wait()


def gather_rows(src, idx):
    n_out = idx.shape[0]
    steps = n_out // GATHER_ROWS
    return pl.pallas_call(
        _gather_kernel,
        out_shape=jax.ShapeDtypeStruct((n_out, src.shape[1]), src.dtype),
        grid=(steps,),
        in_specs=[pl.BlockSpec(memory_space=pl.ANY), pl.BlockSpec(memory_space=pl.ANY)],
        out_specs=pl.BlockSpec(memory_space=pl.ANY),
        scratch_shapes=[pltpu.SMEM((GATHER_ROWS,), jnp.int32), pltpu.SemaphoreType.DMA, pltpu.SemaphoreType.DMA],
        compiler_params=_params(("arbitrary",)),
        name="gather_rows",
    )(idx.reshape(steps, GATHER_ROWS), src)


def _expert_kernel(blk_e, blk_valid, x_ref, w1, w3, w2, y_ref):
    i = pl.program_id(0)

    @pl.when(blk_valid[i] > 0)
    def _():
        x = x_ref[...].astype(BF)
        a = _silu(jnp.dot(x, w1[...], preferred_element_type=F32)) * jnp.dot(x, w3[...], preferred_element_type=F32)
        y_ref[...] = jnp.dot(a.astype(BF), w2[...], preferred_element_type=F32)

    @pl.when(blk_valid[i] == 0)
    def _():
        y_ref[...] = jnp.zeros(y_ref.shape, y_ref.dtype)


def expert_ffn(xs, blk_e, blk_valid, w1, w3, w2):
    cap = xs.shape[0]
    bm = MOE_BLOCK_ROWS
    return pl.pallas_call(
        _expert_kernel,
        out_shape=jax.ShapeDtypeStruct((cap, D_MODEL), F32),
        grid_spec=pltpu.PrefetchScalarGridSpec(
            num_scalar_prefetch=2,
            grid=(cap // bm,),
            in_specs=[pl.BlockSpec((bm, D_MODEL), lambda i, be, bv: (i, 0)),
                      pl.BlockSpec((None, D_MODEL, EXPERT_FF), lambda i, be, bv: (be[i], 0, 0)),
                      pl.BlockSpec((None, D_MODEL, EXPERT_FF), lambda i, be, bv: (be[i], 0, 0)),
                      pl.BlockSpec((None, EXPERT_FF, D_MODEL), lambda i, be, bv: (be[i], 0, 0))],
            out_specs=pl.BlockSpec((bm, D_MODEL), lambda i, be, bv: (i, 0))),
        compiler_params=_params(("arbitrary",)),
        name="expert_ffn",
    )(blk_e, blk_valid, xs, w1, w3, w2)


def _combine_kernel(x_ref, h_ref, y0, y1, y2, y3, y4, y5, wt_ref, sw1, sw3, sw2, g_ref, o_ref):
    h = h_ref[...].astype(BF)
    a = _silu(jnp.dot(h, sw1[...], preferred_element_type=F32)) * jnp.dot(h, sw3[...], preferred_element_type=F32)
    shared = jnp.dot(a.astype(BF), sw2[...], preferred_element_type=F32)
    routed = None
    for k, y in enumerate((y0, y1, y2, y3, y4, y5)):
        term = y[...] * wt_ref[:, k:k + 1]
        routed = term if routed is None else routed + term
    o_ref[...] = x_ref[...] + g_ref[...] * (routed + shared)


def moe_combine(x, h, yg, wts_t, sw1, sw3, sw2, gate, n_prompt):
    t = x.shape[0]
    tm = 256
    nt = t // tm
    row = functools.partial(_mod_row, tm=tm, n_prompt=n_prompt)
    tile = pl.BlockSpec((tm, D_MODEL), lambda i: (i, 0))
    return pl.pallas_call(
        _combine_kernel,
        out_shape=jax.ShapeDtypeStruct((t, D_MODEL), F32),
        grid=(nt,),
        in_specs=([tile, tile]
                  + [pl.BlockSpec((tm, D_MODEL), lambda i, k=k: (k * nt + i, 0)) for k in range(TOP_K)]
                  + [pl.BlockSpec((tm, 8), lambda i: (i, 0)),
                     pl.BlockSpec((D_MODEL, EXPERT_FF), lambda i: (0, 0)),
                     pl.BlockSpec((D_MODEL, EXPERT_FF), lambda i: (0, 0)),
                     pl.BlockSpec((EXPERT_FF, D_MODEL), lambda i: (0, 0)),
                     pl.BlockSpec((None, 1, D_MODEL), lambda i: (row(i), 0, 0))]),
        out_specs=tile,
        compiler_params=_params(("parallel",)),
        name="moe_combine",
    )(x, h, yg, yg, yg, yg, yg, yg, wts_t, sw1, sw3, sw2, gate)


def moe_dispatch_plan(eidx):
    t_n = eidx.shape[0]
    bm = MOE_BLOCK_ROWS
    n_pairs = t_n * TOP_K
    n_blocks = n_pairs // bm + N_EXPERTS
    flat_e = eidx.reshape(-1)
    counts = jnp.zeros((N_EXPERTS,), jnp.int32).at[flat_e].add(1)
    cnt_start = jnp.cumsum(counts) - counts
    pad_counts = (counts + bm - 1) // bm * bm
    pad_end = jnp.cumsum(pad_counts)
    pad_start = pad_end - pad_counts
    order = jnp.argsort(flat_e, stable=True).astype(jnp.int32)
    se = flat_e[order]
    dest = pad_start[se] + jnp.arange(n_pairs, dtype=jnp.int32) - cnt_start[se]
    row_tok = jnp.zeros((n_blocks * bm,), jnp.int32).at[dest].set(order // TOP_K)
    pos = jnp.zeros((n_pairs,), jnp.int32).at[order].set(dest)
    blk_start = jnp.arange(n_blocks, dtype=jnp.int32) * bm
    blk_e = jnp.minimum(jnp.searchsorted(pad_end, blk_start, side="right"), N_EXPERTS - 1).astype(jnp.int32)
    blk_valid = (blk_start < pad_end[-1]).astype(jnp.int32)
    return row_tok, pos, blk_e, blk_valid


def moe_ffn(x, h, prm, layer, gate, n_prompt):
    t_n = h.shape[0]
    e8, w8 = moe_route(h, prm["router_w"][layer], prm["router_bias"][layer])
    row_tok, pos, blk_e, blk_valid = moe_dispatch_plan(e8[:TOP_K].T)
    xs = gather_rows(h, row_tok)
    ys = expert_ffn(xs, blk_e, blk_valid, prm["exp_w1"][layer].astype(BF), prm["exp_w3"][layer].astype(BF),
                    prm["exp_w2"][layer].astype(BF))
    yg = gather_rows(ys, pos.reshape(t_n, TOP_K).T.reshape(-1))
    return moe_combine(x, h, yg, w8.T, prm["sh_w1"][layer].astype(BF), prm["sh_w3"][layer].astype(BF),
                       prm["sh_w2"][layer].astype(BF), gate, n_prompt)


def _split_w_in(w):
    offs = np.concatenate([[0], np.cumsum(IN_SPLITS)])
    dn_qkv, dn_z, dn_b, dn_a, ssd_z, ssd_xbc, ssd_dt, hy_u, na_qkv, gates = [
        w[:, int(offs[i]):int(offs[i + 1])] for i in range(len(IN_SPLITS))]
    small = jnp.concatenate([dn_b, dn_a, ssd_dt], axis=1)
    small = jnp.pad(small, ((0, 0), (0, LANES - small.shape[1])))
    cast = lambda a: a.astype(BF)
    return dict(dn=cast(jnp.concatenate([dn_qkv, dn_z], axis=1)), ssd=cast(jnp.concatenate([ssd_xbc, ssd_z], axis=1)),
                hy=cast(hy_u), na=cast(na_qkv), gates=cast(gates), small=cast(small))


def kernel(x_prompt, x_sample, cache_k, cache_v, state_delta, state_ssd, c, c_ctx,
           ada_w, ada_b, norm_w, final_norm_w, w_in,
           dn_conv_w, dn_conv_b, dn_A_log, dn_dt_bias, dn_norm_w,
           ssd_conv_w, ssd_conv_b, ssd_A_log, ssd_dt_bias, ssd_D, ssd_norm_w,
           hy_conv_w, hy_conv_b, hy_f1_w, hy_f1_b, hy_f2_w, hy_f2_b, hy_f3_w, hy_freq, hy_decay, hy_bias,
           na_rpb, w_branch, w_out,
           router_w, router_bias, exp_w1, exp_w3, exp_w2, sh_w1, sh_w3, sh_w2):
    prm = dict(hy_conv_w=hy_conv_w, hy_conv_b=hy_conv_b, hy_f1_w=hy_f1_w, hy_f1_b=hy_f1_b, hy_f2_w=hy_f2_w,
               hy_f2_b=hy_f2_b, hy_f3_w=hy_f3_w, hy_freq=hy_freq, hy_decay=hy_decay, hy_bias=hy_bias,
               router_w=router_w, router_bias=router_bias, exp_w1=exp_w1, exp_w3=exp_w3, exp_w2=exp_w2,
               sh_w1=sh_w1, sh_w3=sh_w3, sh_w2=sh_w2)
    bp, sp, d = x_prompt.shape
    bs, ss, _ = x_sample.shape
    n_prompt = bp * sp
    x = jnp.concatenate([x_prompt.reshape(n_prompt, d), x_sample.reshape(bs * ss, d)], axis=0)
    t_n = x.shape[0]
    cvecs = jnp.zeros((8, d), F32).at[0].set(c_ctx).at[1:1 + bs].set(c)
    mod = ada_modulation(cvecs, ada_w, ada_b).reshape(DEPTH, 8, 6, 1, d)
    ck = cache_k.reshape(bs, DEPTH, cache_k.shape[2], NA_W)
    cv = cache_v.reshape(bs, DEPTH, cache_v.shape[2], NA_W)
    sd0 = state_delta
    ss0 = state_ssd.reshape(bs, DEPTH, 2, SSD_XW, SSD_N)
    passes = ((sp, 0, bp, True), (ss, n_prompt, bs, False))
    new_k, new_v, new_dn, new_ssd = [], [], [], []
    for l in range(DEPTH):
        sh1, sc1, g1, sh2, sc2, g2 = [mod[l, :, n] for n in range(6)]
        h = norm_modulate(x, norm_w[l, 0], sc1, sh1, n_prompt, BF)
        w = _split_w_in(w_in[l])
        dn_proj = matmul(h, w["dn"], BF)
        ssd_proj = matmul(h, w["ssd"], BF)
        hy_proj = matmul(h, w["hy"], BF)
        na_proj = matmul(h, w["na"], BF)
        gate_proj = matmul(h, w["gates"], BF, tn=1024)
        pcol = matmul(h, w["small"], F32)
        prow = pcol.reshape(t_n // CHUNK, CHUNK, LANES).transpose(0, 2, 1)
        pc_par, pr_par = gate_params(dn_A_log[l], dn_dt_bias[l], ssd_A_log[l], ssd_dt_bias[l])
        bias_tab = neighbourhood_bias_table(na_rpb[l])
        o_a, o_b, o_c, o_d = [], [], [], []
        for seq, row0, batch, is_ctx in passes:
            qkvc = conv_silu(dn_proj, dn_conv_w[l], dn_conv_b[l], DN_QKV, seq, row0, batch)
            of, ob, dn_fin = deltanet_scan(qkvc, pcol, prow, pc_par, pr_par, None if is_ctx else sd0,
                                           l, seq, row0, batch)
            o_a.append(deltanet_out(of, ob, dn_proj, dn_norm_w[l], row0))
            xbcc = conv_silu(ssd_proj, ssd_conv_w[l], ssd_conv_b[l], SSD_XBC, seq, row0, batch)
            yf, yb, ssd_fin = ssd_scan(xbcc, pcol, prow, pc_par, pr_par, None if is_ctx else ss0,
                                       l, seq, row0, batch)
            o_b.append(ssd_out(yf, yb, xbcc, ssd_proj, ssd_D[l], ssd_norm_w[l], row0))
            o_c.append(hyena_mixer(hy_proj, prm, l, seq, row0, batch))
            if is_ctx:
                o_d.append(context_attention(na_proj, seq, batch))
                new_k.append(na_proj[:n_prompt, NA_W:2 * NA_W].astype(F32).reshape(bp, sp, NA_HEADS, NA_HD))
                new_v.append(na_proj[:n_prompt, 2 * NA_W:].astype(F32).reshape(bp, sp, NA_HEADS, NA_HD))
                new_dn.append(dn_fin)
                new_ssd.append(ssd_fin.reshape(bp, 2, SSD_HEADS, SSD_P, SSD_N))
            else:
                o_d.append(neighbourhood_attention(na_proj, ck, cv, bias_tab, l, seq, row0, batch))
        branches = [jnp.concatenate(o, axis=0) for o in (o_a, o_b, o_c, o_d)]
        merged = branch_merge(branches, gate_proj, w_branch[l].astype(BF))
        x = matmul_residual(merged, w_out[l].astype(BF), x, g1, n_prompt)
        h2 = norm_modulate(x, norm_w[l, 1], sc2, sh2, n_prompt, F32)
        x = moe_ffn(x, h2, prm, l, g2, n_prompt)
    y = final_norm(x, final_norm_w)
    return (y[:n_prompt].reshape(bp, sp, d), y[n_prompt:].reshape(bs, ss, d),
            jnp.stack(new_k, axis=1), jnp.stack(new_v, axis=1), jnp.stack(new_dn, axis=1), jnp.stack(new_ssd, axis=1))
```

```python
import functools
import math

import jax
import jax.numpy as jnp
import numpy as np
from jax import lax
from jax.experimental import pallas as pl
from jax.experimental.pallas import tpu as pltpu

BF = jnp.bfloat16
F32 = jnp.float32

D_MODEL = 2048
DEPTH = 2
DEC_SEQ = 4096
GRID_W = 64
EPS = 1e-6
NEG_INF = -1e30
N_BRANCH = 4
DN_HEADS = 4
DN_DK = 128
DN_DV = 128
CHUNK = 64
SSD_HEADS = 8
SSD_P = 64
SSD_GROUPS = 2
SSD_N = 128
HY_WIDTH = 512
HY_BANDS = 16
NA_HEADS = 8
NA_HD = 64
NA_WIN_R = 8
NA_WIN_C = 16
BRANCH_W = 512
N_EXPERTS = 64
TOP_K = 6
N_EXP_GROUPS = 8
TOPK_GROUPS = 4
EXPERT_FF = 512
ROUTED_SCALE = 2.5

DN_QKV = 2 * DN_HEADS * DN_DK + DN_HEADS * DN_DV
SSD_XBC = SSD_HEADS * SSD_P + 2 * SSD_GROUPS * SSD_N
IN_SPLITS = (DN_QKV, DN_HEADS * DN_DV, 2 * DN_HEADS, 2 * DN_HEADS,
             SSD_HEADS * SSD_P, SSD_XBC, 2 * SSD_HEADS,
             3 * HY_WIDTH, 3 * NA_HEADS * NA_HD, N_BRANCH * D_MODEL)

VMEM_LIMIT_BYTES = 56 * 1024 * 1024
LANES = 128
COL_DN_B = 0
COL_DN_A = 2 * DN_HEADS
COL_SSD_DT = 4 * DN_HEADS
CHUNK_GROUP = 4
MOE_BLOCK_ROWS = 512


def _params(sem):
    return pltpu.CompilerParams(dimension_semantics=sem, vmem_limit_bytes=VMEM_LIMIT_BYTES)


def _sigmoid(x):
    return jax.nn.sigmoid(x)


def _silu(x):
    return x * jax.nn.sigmoid(x)


def _dot(a, b):
    return jnp.dot(a.astype(BF), b.astype(BF), preferred_element_type=F32)


def _dot_nt(a, b):
    return lax.dot_general(a.astype(BF), b.astype(BF), (((1,), (1,)), ((), ())), preferred_element_type=F32)


def _dot_tn(a, b):
    return lax.dot_general(a.astype(BF), b.astype(BF), (((0,), (0,)), ((), ())), preferred_element_type=F32)


def _dot_f32(a, b):
    return jnp.dot(a, b, preferred_element_type=F32, precision=lax.Precision.HIGHEST)


def _mod_kernel(c_ref, w_ref, b_ref, o_ref):
    a = _silu(c_ref[...])
    o_ref[...] = _dot(a, w_ref[...]) + b_ref[...]


def ada_modulation(cvecs, ada_w, ada_b):
    n = ada_w.shape[-1]
    tn = 1024
    return pl.pallas_call(
        _mod_kernel,
        out_shape=jax.ShapeDtypeStruct((DEPTH, 8, n), F32),
        grid=(DEPTH, n // tn),
        in_specs=[pl.BlockSpec((8, D_MODEL), lambda l, j: (0, 0)),
                  pl.BlockSpec((None, D_MODEL, tn), lambda l, j: (l, 0, j)),
                  pl.BlockSpec((None, 1, tn), lambda l, j: (l, 0, j))],
        out_specs=pl.BlockSpec((None, 8, tn), lambda l, j: (l, 0, j)),
        compiler_params=_params(("parallel", "parallel")),
        name="ada_modulation",
    )(cvecs, ada_w, ada_b.reshape(DEPTH, 1, n))


def _mod_row(i, tm, n_prompt):
    start = i * tm
    return jnp.where(start < n_prompt, 0, 1 + (start - n_prompt) // DEC_SEQ)


def _norm_mod_kernel(x_ref, w_ref, sc_ref, sh_ref, o_ref):
    x = x_ref[...]
    y = x * lax.rsqrt(jnp.mean(x * x, axis=-1, keepdims=True) + EPS)
    o_ref[...] = ((y * w_ref[...]) * (1.0 + sc_ref[...]) + sh_ref[...]).astype(o_ref.dtype)


def _norm_kernel(x_ref, w_ref, o_ref):
    x = x_ref[...]
    y = x * lax.rsqrt(jnp.mean(x * x, axis=-1, keepdims=True) + EPS)
    o_ref[...] = (y * w_ref[...]).astype(o_ref.dtype)


ROW_SLABS = D_MODEL // LANES


def _to_slabs(ref, val):
    for s in range(ROW_SLABS):
        ref[:, s, :] = val[:, s * LANES:(s + 1) * LANES]


def _from_slabs(ref):
    return jnp.concatenate([ref[:, s, :] for s in range(ROW_SLABS)], axis=-1)


def _norm_mod_slab_kernel(x_ref, w_ref, sc_ref, sh_ref, o_ref, slab_ref):
    x = x_ref[...]
    y = x * lax.rsqrt(jnp.mean(x * x, axis=-1, keepdims=True) + EPS)
    h = (y * w_ref[...]) * (1.0 + sc_ref[...]) + sh_ref[...]
    o_ref[...] = h
    _to_slabs(slab_ref, h)


def norm_modulate(x, w, scale, shift, n_prompt, out_dtype, with_slabs=False):
    t = x.shape[0]
    tm = 512
    row = functools.partial(_mod_row, tm=tm, n_prompt=n_prompt)
    out_shape = jax.ShapeDtypeStruct((t, D_MODEL), out_dtype)
    out_specs = pl.BlockSpec((tm, D_MODEL), lambda i: (i, 0))
    if with_slabs:
        out_shape = (out_shape, jax.ShapeDtypeStruct((t, ROW_SLABS, LANES), F32))
        out_specs = (out_specs, pl.BlockSpec((tm, ROW_SLABS, LANES), lambda i: (i, 0, 0)))
    return pl.pallas_call(
        _norm_mod_slab_kernel if with_slabs else _norm_mod_kernel,
        out_shape=out_shape,
        grid=(t // tm,),
        in_specs=[pl.BlockSpec((tm, D_MODEL), lambda i: (i, 0)),
                  pl.BlockSpec((1, D_MODEL), lambda i: (0, 0)),
                  pl.BlockSpec((None, 1, D_MODEL), lambda i: (row(i), 0, 0)),
                  pl.BlockSpec((None, 1, D_MODEL), lambda i: (row(i), 0, 0))],
        out_specs=out_specs,
        compiler_params=_params(("parallel",)),
        name="norm_modulate",
    )(x, w.reshape(1, D_MODEL), scale, shift)


def final_norm(x, w):
    t = x.shape[0]
    tm = 512
    return pl.pallas_call(
        _norm_kernel,
        out_shape=jax.ShapeDtypeStruct((t, D_MODEL), F32),
        grid=(t // tm,),
        in_specs=[pl.BlockSpec((tm, D_MODEL), lambda i: (i, 0)),
                  pl.BlockSpec((1, D_MODEL), lambda i: (0, 0))],
        out_specs=pl.BlockSpec((tm, D_MODEL), lambda i: (i, 0)),
        compiler_params=_params(("parallel",)),
        name="final_norm",
    )(x, w.reshape(1, D_MODEL))


def _mm_kernel(a_ref, w_ref, o_ref):
    o_ref[...] = jnp.dot(a_ref[...], w_ref[...], preferred_element_type=F32).astype(o_ref.dtype)


def _mm_residual_kernel(a_ref, w_ref, x_ref, g_ref, o_ref):
    acc = jnp.dot(a_ref[...], w_ref[...], preferred_element_type=F32)
    o_ref[...] = x_ref[...] + g_ref[...] * acc


def matmul(a, w, out_dtype, tm=1024, tn=512):
    m, k = a.shape
    n = w.shape[1]
    tn = min(tn, n)
    return pl.pallas_call(
        _mm_kernel,
        out_shape=jax.ShapeDtypeStruct((m, n), out_dtype),
        grid=(m // tm, n // tn),
        in_specs=[pl.BlockSpec((tm, k), lambda i, j: (i, 0)),
                  pl.BlockSpec((k, tn), lambda i, j: (0, j))],
        out_specs=pl.BlockSpec((tm, tn), lambda i, j: (i, j)),
        compiler_params=_params(("parallel", "arbitrary")),
        name="matmul",
    )(a, w)


def matmul_residual(a, w, x, gate, n_prompt, tm=1024, tn=512):
    m, k = a.shape
    n = w.shape[1]
    row = functools.partial(_mod_row, tm=tm, n_prompt=n_prompt)
    return pl.pallas_call(
        _mm_residual_kernel,
        out_shape=jax.ShapeDtypeStruct((m, n), F32),
        grid=(m // tm, n // tn),
        in_specs=[pl.BlockSpec((tm, k), lambda i, j: (i, 0)),
                  pl.BlockSpec((k, tn), lambda i, j: (0, j)),
                  pl.BlockSpec((tm, tn), lambda i, j: (i, j)),
                  pl.BlockSpec((None, 1, tn), lambda i, j: (row(i), 0, j))],
        out_specs=pl.BlockSpec((tm, tn), lambda i, j: (i, j)),
        compiler_params=_params(("parallel", "arbitrary")),
        name="matmul_residual",
    )(a, w, x, gate)


def _conv3(x, w, b):
    length = x.shape[0]
    row = lax.broadcasted_iota(jnp.int32, x.shape, 0)
    prev = jnp.where(row == 0, 0.0, pltpu.roll(x, 1, 0))
    nxt = jnp.where(row == length - 1, 0.0, pltpu.roll(x, length - 1, 0))
    return prev * w[0:1] + x * w[1:2] + nxt * w[2:3] + b


def _conv_silu_kernel(u_ref, w_ref, b_ref, o_ref):
    y = _conv3(u_ref[...].astype(F32), w_ref[...], b_ref[...])
    o_ref[...] = _silu(y).astype(o_ref.dtype)


def conv_silu(proj, conv_w, conv_b, width, seq, row0, batch):
    ct = 128 if seq > 1024 else 512
    rb0 = row0 // seq
    return pl.pallas_call(
        _conv_silu_kernel,
        out_shape=jax.ShapeDtypeStruct((batch * seq, width), BF),
        grid=(batch, width // ct),
        in_specs=[pl.BlockSpec((seq, ct), lambda b, j: (rb0 + b, j)),
                  pl.BlockSpec((3, ct), lambda b, j: (0, j)),
                  pl.BlockSpec((1, ct), lambda b, j: (0, j))],
        out_specs=pl.BlockSpec((seq, ct), lambda b, j: (b, j)),
        compiler_params=_params(("parallel", "parallel")),
        name="conv_silu",
    )(proj, conv_w, conv_b.reshape(1, width))


def _chunk_masks():
    ii = lax.broadcasted_iota(jnp.int32, (CHUNK, CHUNK), 0)
    jj = lax.broadcasted_iota(jnp.int32, (CHUNK, CHUNK), 1)
    return ii, jj


def _log_decay(p, alog, dtb):
    return -jnp.exp(alog) * jax.nn.softplus(p + dtb)


def _cumulative(gcol, grow, lm, um, reverse):
    if not reverse:
        return _dot_f32(lm, gcol), _dot_f32(grow, um)
    return _dot_f32(um, gcol), _dot_f32(grow, lm)


def _inv_unit_triangular(a, eye):
    p = eye - a
    ak = a
    for _ in range(5):
        ak = _dot_f32(ak, ak)
        p = p + _dot_f32(p, ak)
    return p


def _dn_scan_kernel(*refs, zero_init):
    if zero_init:
        (qf, kf, vf, pcf, prf, qb, kb_, vb, pcb, prb, pc_par, pr_par, of, ob, sref) = refs
        s0 = None
    else:
        (qf, kf, vf, pcf, prf, qb, kb_, vb, pcb, prb, pc_par, pr_par, s0, of, ob, sref) = refs
    g = pl.program_id(1)

    @pl.when(g == 0)
    def _():
        if zero_init:
            sref[...] = jnp.zeros(sref.shape, F32)
        else:
            sref[...] = s0[...]

    ii, jj = _chunk_masks()
    eye = (ii == jj).astype(F32)
    lm = (ii >= jj).astype(F32)
    um = (ii <= jj).astype(F32)
    alog_c = pc_par[0:1, :]
    dtb_c = pc_par[1:2, :]
    alog_r = pr_par[:, 0:1]
    dtb_r = pr_par[:, 1:2]
    dirs = ((qf, kf, vf, pcf, prf, of), (qb, kb_, vb, pcb, prb, ob))

    def chunk_step(c, carry):
        for d in range(2):
            q_ref, k_ref, v_ref, pc_ref, pr_ref, o_ref = dirs[d]
            cc = c if d == 0 else CHUNK_GROUP - 1 - c
            r0 = pl.multiple_of(cc * CHUNK, CHUNK)
            pc = pc_ref[pl.ds(r0, CHUNK), :]
            pr = pr_ref[cc]
            gc_col, gc_row = _cumulative(_log_decay(pc, alog_c, dtb_c), _log_decay(pr, alog_r, dtb_r),
                                         lm, um, reverse=(d == 1))
            beta_all = _sigmoid(pc)
            incl = (ii >= jj) if d == 0 else (ii <= jj)
            strict = (ii > jj) if d == 0 else (ii < jj)
            last = CHUNK - 1 if d == 0 else 0
            for h in range(DN_HEADS):
                ci = d * DN_HEADS + h
                sl = slice(h * DN_DK, (h + 1) * DN_DK)
                q = q_ref[pl.ds(r0, CHUNK), sl].astype(F32)
                k = k_ref[pl.ds(r0, CHUNK), sl].astype(F32)
                v = v_ref[pl.ds(r0, CHUNK), sl].astype(F32)
                q = q * lax.rsqrt(jnp.sum(q * q, axis=-1, keepdims=True) + EPS) * (DN_DK ** -0.5)
                k = k * lax.rsqrt(jnp.sum(k * k, axis=-1, keepdims=True) + EPS)
                gcc = gc_col[:, COL_DN_A + ci:COL_DN_A + ci + 1]
                gcr = gc_row[COL_DN_A + ci:COL_DN_A + ci + 1, :]
                glast = gc_col[last:last + 1, COL_DN_A + ci:COL_DN_A + ci + 1]
                decay = jnp.exp(jnp.where(incl, gcc - gcr, -jnp.inf))
                beta = beta_all[:, COL_DN_B + ci:COL_DN_B + ci + 1]
                kbeta = k * beta
                a_mat = jnp.where(strict, _dot_nt(kbeta, k) * decay, 0.0)
                t_mat = _inv_unit_triangular(a_mat, eye)
                egc = jnp.exp(gcc)
                sol = _dot_f32(t_mat, jnp.concatenate([v * beta, kbeta * egc], axis=-1))
                u = sol[:, :DN_DV]
                w = sol[:, DN_DV:]
                s = sref[d, h]
                v_new = u - _dot(w, s)
                attn = _dot_nt(q, k) * decay
                o_ref[pl.ds(r0, CHUNK), sl] = _dot(q * egc, s) + _dot(attn, v_new)
                sref[d, h] = s * jnp.exp(glast) + _dot_tn(k * jnp.exp(glast - gcc), v_new)
        return carry

    lax.fori_loop(0, CHUNK_GROUP, chunk_step, 0)


def deltanet_scan(qkv, pcol, prow, pc_par, pr_par, s0, layer, seq, row0, batch):
    gt = CHUNK_GROUP * CHUNK
    ncg = seq // gt
    hw = DN_HEADS * DN_DK
    pc0 = row0 // gt
    zero_init = s0 is None

    def fwd(b, g):
        return b * ncg + g

    def bwd(b, g):
        return b * ncg + ncg - 1 - g

    def qkv_specs(pos):
        return [pl.BlockSpec((gt, hw), lambda b, g, n=n: (pos(b, g), n)) for n in range(3)]

    def gate_specs(pos):
        return [pl.BlockSpec((gt, LANES), lambda b, g: (pc0 + pos(b, g), 0)),
                pl.BlockSpec((CHUNK_GROUP, LANES, CHUNK), lambda b, g: (pc0 + pos(b, g), 0, 0))]

    in_specs = (qkv_specs(fwd) + gate_specs(fwd) + qkv_specs(bwd) + gate_specs(bwd)
                + [pl.BlockSpec((8, LANES), lambda b, g: (0, 0)),
                   pl.BlockSpec((LANES, 8), lambda b, g: (0, 0))])
    args = [qkv, qkv, qkv, pcol, prow, qkv, qkv, qkv, pcol, prow, pc_par, pr_par]
    if not zero_init:
        in_specs.append(pl.BlockSpec((None, None, 2, DN_HEADS, DN_DK, DN_DV),
                                     lambda b, g: (b, layer, 0, 0, 0, 0)))
        args.append(s0)
    return pl.pallas_call(
        functools.partial(_dn_scan_kernel, zero_init=zero_init),
        out_shape=(jax.ShapeDtypeStruct((batch * seq, hw), F32),
                   jax.ShapeDtypeStruct((batch * seq, hw), F32),
                   jax.ShapeDtypeStruct((batch, 2, DN_HEADS, DN_DK, DN_DV), F32)),
        grid=(batch, ncg),
        in_specs=in_specs,
        out_specs=(pl.BlockSpec((gt, hw), lambda b, g: (fwd(b, g), 0)),
                   pl.BlockSpec((gt, hw), lambda b, g: (bwd(b, g), 0)),
                   pl.BlockSpec((None, 2, DN_HEADS, DN_DK, DN_DV), lambda b, g: (b, 0, 0, 0, 0))),
        compiler_params=_params(("parallel", "arbitrary")),
        name="deltanet_scan",
    )(*args)


def _dn_out_kernel(of_ref, ob_ref, z_ref, w_ref, o_ref):
    for h in range(DN_HEADS):
        sl = slice(h * DN_DV, (h + 1) * DN_DV)
        s = of_ref[:, sl] + ob_ref[:, sl]
        y = s * lax.rsqrt(jnp.mean(s * s, axis=-1, keepdims=True) + EPS) * w_ref[...]
        o_ref[:, sl] = (y * _silu(z_ref[:, sl].astype(F32))).astype(o_ref.dtype)


def deltanet_out(o_f, o_b, proj, norm_w, row0):
    n = o_f.shape[0]
    tm = 512
    hw = DN_HEADS * DN_DV
    return pl.pallas_call(
        _dn_out_kernel,
        out_shape=jax.ShapeDtypeStruct((n, hw), BF),
        grid=(n // tm,),
        in_specs=[pl.BlockSpec((tm, hw), lambda i: (i, 0)),
                  pl.BlockSpec((tm, hw), lambda i: (i, 0)),
                  pl.BlockSpec((tm, hw), lambda i: (row0 // tm + i, DN_QKV // hw)),
                  pl.BlockSpec((1, DN_DV), lambda i: (0, 0))],
        out_specs=pl.BlockSpec((tm, hw), lambda i: (i, 0)),
        compiler_params=_params(("parallel",)),
        name="deltanet_out",
    )(o_f, o_b, proj, norm_w.reshape(1, DN_DV))


def gate_params(dn_a_log, dn_dt_bias, ssd_a_log, ssd_dt_bias):
    def lay(dn, ssd):
        v = jnp.zeros((LANES,), F32)
        v = v.at[COL_DN_A:COL_DN_A + 2 * DN_HEADS].set(dn.reshape(-1).astype(F32))
        return v.at[COL_SSD_DT:COL_SSD_DT + 2 * SSD_HEADS].set(ssd.reshape(-1).astype(F32))
    par = jnp.zeros((8, LANES), F32).at[0].set(lay(dn_a_log, ssd_a_log)).at[1].set(lay(dn_dt_bias, ssd_dt_bias))
    return par, par.T


SSD_XW = SSD_HEADS * SSD_P
SSD_GW = SSD_GROUPS * SSD_N
HEADS_PER_GROUP = SSD_HEADS // SSD_GROUPS


def _ssd_scan_kernel(*refs, zero_init):
    if zero_init:
        (xf, pcf, prf, xb, pcb, prb, pc_par, pr_par, of, ob, sref) = refs
        s0 = None
    else:
        (xf, pcf, prf, xb, pcb, prb, pc_par, pr_par, s0, of, ob, sref) = refs
    g = pl.program_id(1)

    @pl.when(g == 0)
    def _():
        if zero_init:
            sref[...] = jnp.zeros(sref.shape, F32)
        else:
            sref[...] = s0[...]

    ii, jj = _chunk_masks()
    lm = (ii >= jj).astype(F32)
    um = (ii <= jj).astype(F32)
    alog_c = pc_par[0:1, :]
    dtb_c = pc_par[1:2, :]
    alog_r = pr_par[:, 0:1]
    dtb_r = pr_par[:, 1:2]
    dirs = ((xf, pcf, prf, of), (xb, pcb, prb, ob))
    gw = HEADS_PER_GROUP * SSD_P

    def chunk_step(c, carry):
        for d in range(2):
            x_ref, pc_ref, pr_ref, o_ref = dirs[d]
            cc = c if d == 0 else CHUNK_GROUP - 1 - c
            r0 = pl.multiple_of(cc * CHUNK, CHUNK)
            rows = pl.ds(r0, CHUNK)
            pc = pc_ref[rows, :]
            pr = pr_ref[cc]
            dt_all = jax.nn.softplus(pc + dtb_c)
            ac_col, ac_row = _cumulative(_log_decay(pc, alog_c, dtb_c), _log_decay(pr, alog_r, dtb_r),
                                         lm, um, reverse=(d == 1))
            incl = (ii >= jj) if d == 0 else (ii <= jj)
            last = CHUNK - 1 if d == 0 else 0
            for gi in range(SSD_GROUPS):
                bm = x_ref[rows, SSD_XW + gi * SSD_N:SSD_XW + (gi + 1) * SSD_N]
                cm = x_ref[rows, SSD_XW + SSD_GW + gi * SSD_N:SSD_XW + SSD_GW + (gi + 1) * SSD_N]
                scores = _dot_nt(cm, bm)
                srow = pl.ds(gi * gw, gw)
                s_grp = sref[d, srow, :]
                y_state = _dot_nt(cm, s_grp)
                ys, xs, decs = [], [], []
                for hh in range(HEADS_PER_GROUP):
                    h = gi * HEADS_PER_GROUP + hh
                    col = COL_SSD_DT + d * SSD_HEADS + h
                    acc = ac_col[:, col:col + 1]
                    acr = ac_row[col:col + 1, :]
                    alast = ac_col[last:last + 1, col:col + 1]
                    decay = jnp.exp(jnp.where(incl, acc - acr, -jnp.inf))
                    xdt = x_ref[rows, h * SSD_P:(h + 1) * SSD_P].astype(F32) * dt_all[:, col:col + 1]
                    ys.append(_dot(scores * decay, xdt) + y_state[:, hh * SSD_P:(hh + 1) * SSD_P] * jnp.exp(acc))
                    xs.append(xdt * jnp.exp(alast - acc))
                    decs.append(jnp.broadcast_to(jnp.exp(alast), (SSD_P, 1)))
                o_ref[rows, gi * gw:(gi + 1) * gw] = jnp.concatenate(ys, axis=-1)
                s_add = _dot_tn(jnp.concatenate(xs, axis=-1), bm)
                sref[d, srow, :] = s_grp * jnp.concatenate(decs, axis=0) + s_add
        return carry

    lax.fori_loop(0, CHUNK_GROUP, chunk_step, 0)


def ssd_scan(xbc, pcol, prow, pc_par, pr_par, s0, layer, seq, row0, batch):
    gt = CHUNK_GROUP * CHUNK
    ncg = seq // gt
    pc0 = row0 // gt
    zero_init = s0 is None

    def fwd(b, g):
        return b * ncg + g

    def bwd(b, g):
        return b * ncg + ncg - 1 - g

    def specs(pos):
        return [pl.BlockSpec((gt, SSD_XBC), lambda b, g: (pos(b, g), 0)),
                pl.BlockSpec((gt, LANES), lambda b, g: (pc0 + pos(b, g), 0)),
                pl.BlockSpec((CHUNK_GROUP, LANES, CHUNK), lambda b, g: (pc0 + pos(b, g), 0, 0))]

    in_specs = specs(fwd) + specs(bwd) + [pl.BlockSpec((8, LANES), lambda b, g: (0, 0)),
                                          pl.BlockSpec((LANES, 8), lambda b, g: (0, 0))]
    args = [xbc, pcol, prow, xbc, pcol, prow, pc_par, pr_par]
    if not zero_init:
        in_specs.append(pl.BlockSpec((None, None, 2, SSD_XW, SSD_N), lambda b, g: (b, layer, 0, 0, 0)))
        args.append(s0)
    return pl.pallas_call(
        functools.partial(_ssd_scan_kernel, zero_init=zero_init),
        out_shape=(jax.ShapeDtypeStruct((batch * seq, SSD_XW), F32),
                   jax.ShapeDtypeStruct((batch * seq, SSD_XW), F32),
                   jax.ShapeDtypeStruct((batch, 2, SSD_XW, SSD_N), F32)),
        grid=(batch, ncg),
        in_specs=in_specs,
        out_specs=(pl.BlockSpec((gt, SSD_XW), lambda b, g: (fwd(b, g), 0)),
                   pl.BlockSpec((gt, SSD_XW), lambda b, g: (bwd(b, g), 0)),
                   pl.BlockSpec((None, 2, SSD_XW, SSD_N), lambda b, g: (b, 0, 0, 0))),
        compiler_params=_params(("parallel", "arbitrary")),
        name="ssd_scan",
    )(*args)


def _ssd_out_kernel(yf_ref, yb_ref, x_ref, z_ref, d_ref, w_ref, o_ref):
    y = yf_ref[...] + yb_ref[...] + x_ref[...].astype(F32) * d_ref[...]
    y = y * _silu(z_ref[...].astype(F32))
    o_ref[...] = (y * lax.rsqrt(jnp.mean(y * y, axis=-1, keepdims=True) + EPS) * w_ref[...]).astype(o_ref.dtype)


def ssd_out(y_f, y_b, xbc, proj, d_skip, norm_w, row0):
    n = y_f.shape[0]
    tm = 512
    return pl.pallas_call(
        _ssd_out_kernel,
        out_shape=jax.ShapeDtypeStruct((n, SSD_XW), BF),
        grid=(n // tm,),
        in_specs=[pl.BlockSpec((tm, SSD_XW), lambda i: (i, 0)),
                  pl.BlockSpec((tm, SSD_XW), lambda i: (i, 0)),
                  pl.BlockSpec((tm, SSD_XW), lambda i: (i, 0)),
                  pl.BlockSpec((tm, SSD_XW), lambda i: (row0 // tm + i, SSD_XBC // SSD_XW)),
                  pl.BlockSpec((1, SSD_XW), lambda i: (0, 0)),
                  pl.BlockSpec((1, SSD_XW), lambda i: (0, 0))],
        out_specs=pl.BlockSpec((tm, SSD_XW), lambda i: (i, 0)),
        compiler_params=_params(("parallel",)),
        name="ssd_out",
    )(y_f, y_b, xbc, proj, jnp.repeat(d_skip.astype(F32), SSD_P).reshape(1, SSD_XW), norm_w.reshape(1, SSD_XW))


HY_FEAT_PAD = 128
HY_HID = 64


def _hy_filter_kernel(feat_ref, f1w, f1b, f2w, f2b, f3w, fq, dec, o_ref):
    feat = feat_ref[...]
    hdn = jnp.sin(fq[...] * (_dot_f32(feat, f1w[...]) + f1b[...]))
    hdn = jnp.sin(fq[...] * (_dot_f32(hdn, f2w[...]) + f2b[...]))
    filt = _dot_f32(hdn, f3w[...])
    window = jnp.exp(-jnp.abs(dec[...]) * (2.0 * jnp.abs(feat[:, 0:1])))
    o_ref[...] = filt * window


def hyena_filter(seq, f1_w, f1_b, f2_w, f2_b, f3_w, freq, decay):
    off = (jnp.arange(seq, dtype=F32) - (seq // 2)) / seq
    ang = 2.0 * math.pi * off[:, None] * jnp.arange(1, HY_BANDS + 1, dtype=F32)
    feat = jnp.concatenate([off[:, None], jnp.sin(ang), jnp.cos(ang)], axis=-1)
    emb = feat.shape[1]
    feat = jnp.pad(feat, ((0, 0), (0, HY_FEAT_PAD - emb)))
    f1p = jnp.pad(f1_w.astype(F32), ((0, HY_FEAT_PAD - emb), (0, 0)))
    tl = min(seq, 512)
    full = lambda shape: pl.BlockSpec(shape, lambda i: (0, 0))
    return pl.pallas_call(
        _hy_filter_kernel,
        out_shape=jax.ShapeDtypeStruct((seq, HY_WIDTH), F32),
        grid=(seq // tl,),
        in_specs=[pl.BlockSpec((tl, HY_FEAT_PAD), lambda i: (i, 0)),
                  full((HY_FEAT_PAD, HY_HID)), full((1, HY_HID)), full((HY_HID, HY_HID)), full((1, HY_HID)),
                  full((HY_HID, HY_WIDTH)), full((1, HY_HID)), full((1, HY_WIDTH))],
        out_specs=pl.BlockSpec((tl, HY_WIDTH), lambda i: (i, 0)),
        compiler_params=_params(("parallel",)),
        name="hyena_filter",
    )(feat, f1p, f1_b.reshape(1, HY_HID), f2_w, f2_b.reshape(1, HY_HID), f3_w,
      freq.reshape(1, HY_HID), decay.reshape(1, HY_WIDTH))


def _hy_pre_kernel(u0, u1, u2, w0, w1, w2, b0, b1, b2, x0_ref, vv_ref):
    x0 = _conv3(u0[...].astype(F32), w0[...], b0[...])
    x1 = _conv3(u1[...].astype(F32), w1[...], b1[...])
    v = _conv3(u2[...].astype(F32), w2[...], b2[...])
    x0_ref[...] = x0.astype(x0_ref.dtype)
    vv_ref[...] = (v * x1).astype(vv_ref.dtype)


def hyena_pre(proj, conv_w, conv_b, seq, row0, batch):
    ct = 128
    nct = HY_WIDTH // ct
    rb0 = row0 // seq
    cb = conv_b.reshape(1, 3 * HY_WIDTH)
    u_specs = [pl.BlockSpec((seq, ct), lambda b, j, n=n: (rb0 + b, n * nct + j)) for n in range(3)]
    w_specs = [pl.BlockSpec((3, ct), lambda b, j, n=n: (0, n * nct + j)) for n in range(3)]
    b_specs = [pl.BlockSpec((1, ct), lambda b, j, n=n: (0, n * nct + j)) for n in range(3)]
    return pl.pallas_call(
        _hy_pre_kernel,
        out_shape=(jax.ShapeDtypeStruct((batch * seq, HY_WIDTH), BF),
                   jax.ShapeDtypeStruct((batch, seq, HY_WIDTH), BF)),
        grid=(batch, nct),
        in_specs=u_specs + w_specs + b_specs,
        out_specs=(pl.BlockSpec((seq, ct), lambda b, j: (b, j)),
                   pl.BlockSpec((None, seq, ct), lambda b, j: (b, 0, j))),
        compiler_params=_params(("parallel", "parallel")),
        name="hyena_pre",
    )(proj, proj, proj, conv_w, conv_w, conv_w, cb, cb, cb)


def _trig_tables(seq):
    n = 2 * seq
    nblk = seq // LANES

    def trig(m):
        ang = (m % (2 * n)).astype(F32) * (math.pi / n)
        return jnp.cos(ang), jnp.sin(ang)

    k = jnp.arange(seq, dtype=jnp.int32)[:, None]
    blk = jnp.arange(nblk, dtype=jnp.int32)[None, :]
    lane = jnp.arange(LANES, dtype=jnp.int32)[None, :]
    ca, sa = trig((2 * k + 1) * (LANES * blk))
    fwd_c0 = jnp.concatenate([ca, -sa], axis=0)
    fwd_s0 = jnp.concatenate([sa, ca], axis=0)
    fwd_cd, fwd_sd = trig((2 * k + 1) * lane)
    tp = k + seq // 2
    ca, sa = trig((2 * LANES * blk + 1) * tp)
    scale = 2.0 / n
    inv_c0 = jnp.concatenate([ca, -sa], axis=1) * scale
    inv_s0 = jnp.concatenate([sa, ca], axis=1) * scale
    inv_cd, inv_sd = trig(2 * lane * tp)
    return (fwd_c0, fwd_s0, fwd_cd, fwd_sd), (inv_c0, inv_s0, inv_cd, inv_sd)


def _trig_tile(dst, r0, rows, c0, s0, cd, sd):
    for j in range(c0.shape[1]):
        dst[r0:r0 + rows, j * LANES:(j + 1) * LANES] = (c0[:, j:j + 1] * cd - s0[:, j:j + 1] * sd).astype(dst.dtype)


def _dft_fwd_kernel(*refs, mul_h, tk):
    if mul_h:
        c0a, s0a, c0b, s0b, cd, sd, x_ref, h_ref, o_ref, lhs = refs
    else:
        c0a, s0a, c0b, s0b, cd, sd, x_ref, o_ref, lhs = refs

    @pl.when(pl.program_id(1) == 0)
    def _():
        _trig_tile(lhs, 0, tk, c0a[...], s0a[...], cd[...], sd[...])
        _trig_tile(lhs, tk, tk, c0b[...], s0b[...], cd[...], sd[...])

    acc = jnp.dot(lhs[...], x_ref[...].astype(BF), preferred_element_type=F32)
    ure = acc[:tk]
    uim = acc[tk:]
    if mul_h:
        hre = h_ref[0]
        him = h_ref[1]
        o_ref[0] = (ure * hre - uim * him).astype(o_ref.dtype)
        o_ref[1] = (ure * him + uim * hre).astype(o_ref.dtype)
    else:
        o_ref[0] = ure.astype(o_ref.dtype)
        o_ref[1] = uim.astype(o_ref.dtype)


def dft_forward(x, tables, h, out_dtype):
    batch, seq, width = x.shape
    c0, s0, cd, sd = tables
    nblk = seq // LANES
    tk = min(256, seq)
    nk = seq // tk
    mul_h = h is not None
    in_specs = [pl.BlockSpec((tk, nblk), lambda i, b: (i, 0)),
                pl.BlockSpec((tk, nblk), lambda i, b: (i, 0)),
                pl.BlockSpec((tk, nblk), lambda i, b: (nk + i, 0)),
                pl.BlockSpec((tk, nblk), lambda i, b: (nk + i, 0)),
                pl.BlockSpec((tk, LANES), lambda i, b: (i, 0)),
                pl.BlockSpec((tk, LANES), lambda i, b: (i, 0)),
                pl.BlockSpec((None, seq, width), lambda i, b: (b, 0, 0))]
    args = [c0, s0, c0, s0, cd, sd, x]
    if mul_h:
        in_specs.append(pl.BlockSpec((None, 2, tk, width), lambda i, b: (0, 0, i, 0)))
        args.append(h)
    return pl.pallas_call(
        functools.partial(_dft_fwd_kernel, mul_h=mul_h, tk=tk),
        out_shape=jax.ShapeDtypeStruct((batch, 2, seq, width), out_dtype),
        grid=(nk, batch),
        in_specs=in_specs,
        out_specs=pl.BlockSpec((None, 2, tk, width), lambda i, b: (b, 0, i, 0)),
        scratch_shapes=[pltpu.VMEM((2 * tk, seq), BF)],
        compiler_params=_params(("parallel", "arbitrary")),
        name="dft_forward",
    )(*args)


def _dft_inv_kernel(c0, s0, cd, sd, y_ref, vv_ref, x0_ref, bias_ref, o_ref, lhs):
    @pl.when(pl.program_id(1) == 0)
    def _():
        _trig_tile(lhs, 0, lhs.shape[0], c0[...], s0[...], cd[...], sd[...])

    conv = jnp.dot(lhs[...], y_ref[...], preferred_element_type=F32)
    vv = vv_ref[...].astype(F32)
    o_ref[...] = ((conv + vv * bias_ref[...]) * x0_ref[...].astype(F32)).astype(o_ref.dtype)


def dft_inverse_gate(y, tables, vv, x0, bias):
    batch, seq, width = vv.shape
    c0, s0, cd, sd = tables
    nblk2 = c0.shape[1]
    tm = min(256, seq)
    nt = seq // tm
    return pl.pallas_call(
        _dft_inv_kernel,
        out_shape=jax.ShapeDtypeStruct((batch * seq, width), BF),
        grid=(nt, batch),
        in_specs=[pl.BlockSpec((tm, nblk2), lambda i, b: (i, 0)),
                  pl.BlockSpec((tm, nblk2), lambda i, b: (i, 0)),
                  pl.BlockSpec((tm, LANES), lambda i, b: (i, 0)),
                  pl.BlockSpec((tm, LANES), lambda i, b: (i, 0)),
                  pl.BlockSpec((None, 2 * seq, width), lambda i, b: (b, 0, 0)),
                  pl.BlockSpec((None, tm, width), lambda i, b: (b, i, 0)),
                  pl.BlockSpec((tm, width), lambda i, b: (b * nt + i, 0)),
                  pl.BlockSpec((1, width), lambda i, b: (0, 0))],
        out_specs=pl.BlockSpec((tm, width), lambda i, b: (b * nt + i, 0)),
        scratch_shapes=[pltpu.VMEM((tm, 2 * seq), BF)],
        compiler_params=_params(("parallel", "arbitrary")),
        name="dft_inverse_gate",
    )(c0, s0, cd, sd, y, vv, x0, bias.reshape(1, width).astype(F32))


def hyena_mixer(proj, prm, layer, seq, row0, batch):
    filt = hyena_filter(seq, prm["hy_f1_w"][layer], prm["hy_f1_b"][layer], prm["hy_f2_w"][layer],
                        prm["hy_f2_b"][layer], prm["hy_f3_w"][layer], prm["hy_freq"][layer],
                        prm["hy_decay"][layer])
    fwd_tab, inv_tab = _trig_tables(seq)
    h_spec = dft_forward(filt[None], fwd_tab, None, F32)
    x0, vv = hyena_pre(proj, prm["hy_conv_w"][layer], prm["hy_conv_b"][layer], seq, row0, batch)
    y_spec = dft_forward(vv, fwd_tab, h_spec, BF)
    return dft_inverse_gate(y_spec.reshape(batch, 2 * seq, HY_WIDTH), inv_tab, vv, x0, prm["hy_bias"][layer])


NA_W = NA_HEADS * NA_HD
NA_SCALE = NA_HD ** -0.5


def _softmax_attend(scores, values):
    m = functools.reduce(jnp.maximum, [jnp.max(s, axis=-1, keepdims=True) for s in scores])
    ps = [jnp.exp(s - m) for s in scores]
    denom = functools.reduce(lambda a, b: a + b, [jnp.sum(p, axis=-1, keepdims=True) for p in ps])
    num = functools.reduce(lambda a, b: a + b, [_dot(p, v) for p, v in zip(ps, values)])
    return num / denom


def _ctx_attn_kernel(q_ref, k_ref, v_ref, o_ref):
    outs = []
    for h in range(NA_HEADS):
        sl = slice(h * NA_HD, (h + 1) * NA_HD)
        s = _dot_nt(q_ref[:, sl], k_ref[:, sl]) * NA_SCALE
        outs.append(_softmax_attend([s], [v_ref[:, sl]]))
    o_ref[...] = jnp.concatenate(outs, axis=-1).astype(o_ref.dtype)


def context_attention(proj, seq, batch):
    return pl.pallas_call(
        _ctx_attn_kernel,
        out_shape=jax.ShapeDtypeStruct((batch * seq, NA_W), BF),
        grid=(batch,),
        in_specs=[pl.BlockSpec((seq, NA_W), lambda b, n=n: (b, n)) for n in range(3)],
        out_specs=pl.BlockSpec((seq, NA_W), lambda b: (b, 0)),
        compiler_params=_params(("parallel",)),
        name="context_attention",
    )(proj, proj, proj)


def _na_row_start(r, rows_n):
    return jnp.clip(r - NA_WIN_R // 2, 0, rows_n - NA_WIN_R)


def _nbr_attn_kernel(q_ref, k_ref, v_ref, kc_ref, vc_ref, bias_ref, o_ref, *, rows_n):
    r = pl.program_id(1)
    koff = pl.multiple_of(_na_row_start(r, rows_n) * GRID_W, GRID_W)
    win = pl.ds(koff, NA_WIN_R * GRID_W)
    outs = []
    for h in range(NA_HEADS):
        sl = slice(h * NA_HD, (h + 1) * NA_HD)
        q = q_ref[:, sl]
        bias = bias_ref[h]
        s_loc = _dot_nt(q, k_ref[win, sl]) * NA_SCALE
        s_loc = jnp.where(bias > 0.5 * NEG_INF, s_loc + bias, NEG_INF)
        s_ctx = _dot_nt(q, kc_ref[:, sl]) * NA_SCALE
        outs.append(_softmax_attend([s_loc, s_ctx], [v_ref[win, sl], vc_ref[:, sl]]))
    o_ref[...] = jnp.concatenate(outs, axis=-1).astype(o_ref.dtype)


def neighbourhood_bias_table(rpb):
    cols = jnp.arange(GRID_W)
    col_start = jnp.clip(cols - NA_WIN_C // 2, 0, GRID_W - NA_WIN_C)
    col_in = (cols[None, :] >= col_start[:, None]) & (cols[None, :] < col_start[:, None] + NA_WIN_C)
    dc = jnp.clip(cols[None, :] - cols[:, None], -(NA_WIN_C - 1), NA_WIN_C - 1) + (NA_WIN_C - 1)
    var = jnp.arange(NA_WIN_R)
    dr = jnp.arange(NA_WIN_R)[None, :] - var[:, None] + (NA_WIN_R - 1)
    tab = rpb.astype(F32)[:, dr[:, None, :, None], dc[None, :, None, :]]
    tab = jnp.where(col_in[None, None, :, None, :], tab, NEG_INF)
    return tab.transpose(1, 0, 2, 3, 4).reshape(NA_WIN_R, NA_HEADS, GRID_W, NA_WIN_R * GRID_W)


def neighbourhood_attention(proj, cache_k, cache_v, bias_tab, layer, seq, row0, batch):
    rows_n = seq // GRID_W
    assert rows_n >= NA_WIN_R
    past = cache_k.shape[2]
    qb0 = row0 // GRID_W
    sb0 = row0 // seq

    def variant(r):
        return r - _na_row_start(r, rows_n)

    return pl.pallas_call(
        functools.partial(_nbr_attn_kernel, rows_n=rows_n),
        out_shape=jax.ShapeDtypeStruct((batch * seq, NA_W), BF),
        grid=(batch, rows_n),
        in_specs=[pl.BlockSpec((GRID_W, NA_W), lambda b, r: (qb0 + b * rows_n + r, 0)),
                  pl.BlockSpec((seq, NA_W), lambda b, r: (sb0 + b, 1)),
                  pl.BlockSpec((seq, NA_W), lambda b, r: (sb0 + b, 2)),
                  pl.BlockSpec((None, None, past, NA_W), lambda b, r: (b, layer, 0, 0)),
                  pl.BlockSpec((None, None, past, NA_W), lambda b, r: (b, layer, 0, 0)),
                  pl.BlockSpec((None, NA_HEADS, GRID_W, NA_WIN_R * GRID_W), lambda b, r: (variant(r), 0, 0, 0))],
        out_specs=pl.BlockSpec((GRID_W, NA_W), lambda b, r: (b * rows_n + r, 0)),
        compiler_params=_params(("parallel", "arbitrary")),
        name="neighbourhood_attention",
    )(proj, proj, proj, cache_k, cache_v, bias_tab)


def _merge_kernel(oa, ob, oc, od, g0, g1, g2, g3, wb, o_ref):
    acc = None
    for n, (o_n, g_n) in enumerate(((oa, g0), (ob, g1), (oc, g2), (od, g3))):
        term = _sigmoid(g_n[...].astype(F32)) * jnp.dot(o_n[...], wb[n], preferred_element_type=F32)
        acc = term if acc is None else acc + term
    o_ref[...] = acc.astype(o_ref.dtype)


def branch_merge(branches, gate_proj, w_branch):
    t = gate_proj.shape[0]
    tm, tn = 512, 1024
    nj = D_MODEL // tn
    return pl.pallas_call(
        _merge_kernel,
        out_shape=jax.ShapeDtypeStruct((t, D_MODEL), BF),
        grid=(t // tm, nj),
        in_specs=([pl.BlockSpec((tm, BRANCH_W), lambda i, j: (i, 0)) for _ in range(N_BRANCH)]
                  + [pl.BlockSpec((tm, tn), lambda i, j, n=n: (i, n * nj + j)) for n in range(N_BRANCH)]
                  + [pl.BlockSpec((N_BRANCH, BRANCH_W, tn), lambda i, j: (0, 0, j))]),
        out_specs=pl.BlockSpec((tm, tn), lambda i, j: (i, j)),
        compiler_params=_params(("parallel", "arbitrary")),
        name="branch_merge",
    )(*branches, gate_proj, gate_proj, gate_proj, gate_proj, w_branch)


EXP_PER_GROUP = N_EXPERTS // N_EXP_GROUPS


def _first_argmax(v, idx, size):
    m = jnp.max(v, axis=0, keepdims=True)
    first = jnp.min(jnp.where(v == m, idx, size), axis=0, keepdims=True)
    return m, first


def _router_kernel(h_ref, rw_ref, b_ref, e_ref, w_ref):
    logits = lax.dot_general(rw_ref[...], h_ref[...], (((1,), (1,)), ((), ())), preferred_element_type=F32,
                             precision=lax.Precision.HIGHEST)
    scores = _sigmoid(logits)
    biased = scores + b_ref[...]
    tm = scores.shape[1]
    sub = lax.broadcasted_iota(jnp.int32, (EXP_PER_GROUP, tm), 0)
    grp = []
    for g in range(N_EXP_GROUPS):
        v = biased[g * EXP_PER_GROUP:(g + 1) * EXP_PER_GROUP]
        m1, first = _first_argmax(v, sub, EXP_PER_GROUP)
        m2 = jnp.max(jnp.where(sub == first, -jnp.inf, v), axis=0, keepdims=True)
        grp.append(m1 + m2)
    cur = jnp.concatenate(grp, axis=0)
    gidx = lax.broadcasted_iota(jnp.int32, (N_EXP_GROUPS, tm), 0)
    sel = jnp.zeros((N_EXP_GROUPS, tm), F32)
    for _ in range(TOPK_GROUPS):
        _, first = _first_argmax(cur, gidx, N_EXP_GROUPS)
        pick = gidx == first
        sel = jnp.where(pick, 1.0, sel)
        cur = jnp.where(pick, -jnp.inf, cur)
    cur = jnp.concatenate(
        [jnp.where(sel[g:g + 1] > 0.5, biased[g * EXP_PER_GROUP:(g + 1) * EXP_PER_GROUP], NEG_INF)
         for g in range(N_EXP_GROUPS)], axis=0)
    eidx = lax.broadcasted_iota(jnp.int32, (N_EXPERTS, tm), 0)
    es, ws = [], []
    for _ in range(TOP_K):
        _, first = _first_argmax(cur, eidx, N_EXPERTS)
        pick = eidx == first
        es.append(first)
        ws.append(jnp.sum(jnp.where(pick, scores, 0.0), axis=0, keepdims=True))
        cur = jnp.where(pick, -jnp.inf, cur)
    total = functools.reduce(lambda a, b: a + b, ws)
    pad = 8 - TOP_K
    e_ref[...] = jnp.concatenate(es + [jnp.zeros((pad, tm), jnp.int32)], axis=0)
    w_ref[...] = jnp.concatenate([w / total * ROUTED_SCALE for w in ws] + [jnp.zeros((pad, tm), F32)], axis=0)


def moe_route(h, router_w, router_bias):
    t = h.shape[0]
    tm = 512
    return pl.pallas_call(
        _router_kernel,
        out_shape=(jax.ShapeDtypeStruct((8, t), jnp.int32), jax.ShapeDtypeStruct((8, t), F32)),
        grid=(t // tm,),
        in_specs=[pl.BlockSpec((tm, D_MODEL), lambda i: (i, 0)),
                  pl.BlockSpec((N_EXPERTS, D_MODEL), lambda i: (0, 0)),
                  pl.BlockSpec((N_EXPERTS, 1), lambda i: (0, 0))],
        out_specs=(pl.BlockSpec((8, tm), lambda i: (0, i)), pl.BlockSpec((8, tm), lambda i: (0, i))),
        compiler_params=_params(("parallel",)),
        name="moe_route",
    )(h, router_w.T.astype(F32), router_bias.reshape(N_EXPERTS, 1).astype(F32))


GATHER_ROWS = 1024


def _gather_kernel(idx_hbm, src, dst, idx_smem, sem_idx, sem_rows):
    i = pl.program_id(0)
    load = pltpu.make_async_copy(idx_hbm.at[i], idx_smem, sem_idx)
    load.start()
    load.wait()
    base = i * GATHER_ROWS

    def issue(r, carry):
        pltpu.make_async_copy(src.at[pl.ds(idx_smem[r], 1)], dst.at[pl.ds(base + r, 1)], sem_rows).start()
        return carry

    lax.fori_loop(0, GATHER_ROWS, issue, 0)
    pltpu.make_async_copy(src.at[pl.ds(0, GATHER_ROWS)], dst.at[pl.ds(base, GATHER_ROWS)], sem_rows).wait()


def gather_rows(src, idx):
    n_out = idx.shape[0]
    steps = n_out // GATHER_ROWS
    return pl.pallas_call(
        _gather_kernel,
        out_shape=jax.ShapeDtypeStruct((n_out,) + src.shape[1:], src.dtype),
        grid=(steps,),
        in_specs=[pl.BlockSpec(memory_space=pl.ANY), pl.BlockSpec(memory_space=pl.ANY)],
        out_specs=pl.BlockSpec(memory_space=pl.ANY),
        scratch_shapes=[pltpu.SMEM((GATHER_ROWS,), jnp.int32), pltpu.SemaphoreType.DMA, pltpu.SemaphoreType.DMA],
        compiler_params=_params(("arbitrary",)),
        name="gather_rows",
    )(idx.reshape(steps, GATHER_ROWS), src)


def _expert_kernel(blk_e, blk_valid, x_ref, w1, w3, w2, y_ref):
    i = pl.program_id(0)

    @pl.when(blk_valid[i] > 0)
    def _():
        x = _from_slabs(x_ref).astype(BF)
        a = _silu(jnp.dot(x, w1[...], preferred_element_type=F32)) * jnp.dot(x, w3[...], preferred_element_type=F32)
        _to_slabs(y_ref, jnp.dot(a.astype(BF), w2[...], preferred_element_type=F32))

    @pl.when(blk_valid[i] == 0)
    def _():
        y_ref[...] = jnp.zeros(y_ref.shape, y_ref.dtype)


def expert_ffn(xs, blk_e, blk_valid, w1, w3, w2):
    cap = xs.shape[0]
    bm = MOE_BLOCK_ROWS
    slab = pl.BlockSpec((bm, ROW_SLABS, LANES), lambda i, be, bv: (i, 0, 0))
    return pl.pallas_call(
        _expert_kernel,
        out_shape=jax.ShapeDtypeStruct((cap, ROW_SLABS, LANES), F32),
        grid_spec=pltpu.PrefetchScalarGridSpec(
            num_scalar_prefetch=2,
            grid=(cap // bm,),
            in_specs=[slab,
                      pl.BlockSpec((None, D_MODEL, EXPERT_FF), lambda i, be, bv: (be[i], 0, 0)),
                      pl.BlockSpec((None, D_MODEL, EXPERT_FF), lambda i, be, bv: (be[i], 0, 0)),
                      pl.BlockSpec((None, EXPERT_FF, D_MODEL), lambda i, be, bv: (be[i], 0, 0))],
            out_specs=slab),
        compiler_params=_params(("arbitrary",)),
        name="expert_ffn",
    )(blk_e, blk_valid, xs, w1, w3, w2)


def _combine_kernel(x_ref, h_ref, y0, y1, y2, y3, y4, y5, wt_ref, sw1, sw3, sw2, g_ref, o_ref):
    h = h_ref[...].astype(BF)
    a = _silu(jnp.dot(h, sw1[...], preferred_element_type=F32)) * jnp.dot(h, sw3[...], preferred_element_type=F32)
    shared = jnp.dot(a.astype(BF), sw2[...], preferred_element_type=F32)
    routed = None
    for k, y in enumerate((y0, y1, y2, y3, y4, y5)):
        term = _from_slabs(y) * wt_ref[:, k:k + 1]
        routed = term if routed is None else routed + term
    o_ref[...] = x_ref[...] + g_ref[...] * (routed + shared)


def moe_combine(x, h, yg, wts_t, sw1, sw3, sw2, gate, n_prompt):
    t = x.shape[0]
    tm = 256
    nt = t // tm
    row = functools.partial(_mod_row, tm=tm, n_prompt=n_prompt)
    tile = pl.BlockSpec((tm, D_MODEL), lambda i: (i, 0))
    return pl.pallas_call(
        _combine_kernel,
        out_shape=jax.ShapeDtypeStruct((t, D_MODEL), F32),
        grid=(nt,),
        in_specs=([tile, tile]
                  + [pl.BlockSpec((tm, ROW_SLABS, LANES), lambda i, k=k: (k * nt + i, 0, 0)) for k in range(TOP_K)]
                  + [pl.BlockSpec((tm, 8), lambda i: (i, 0)),
                     pl.BlockSpec((D_MODEL, EXPERT_FF), lambda i: (0, 0)),
                     pl.BlockSpec((D_MODEL, EXPERT_FF), lambda i: (0, 0)),
                     pl.BlockSpec((EXPERT_FF, D_MODEL), lambda i: (0, 0)),
                     pl.BlockSpec((None, 1, D_MODEL), lambda i: (row(i), 0, 0))]),
        out_specs=tile,
        compiler_params=_params(("parallel",)),
        name="moe_combine",
    )(x, h, yg, yg, yg, yg, yg, yg, wts_t, sw1, sw3, sw2, gate)


def moe_dispatch_plan(eidx):
    t_n = eidx.shape[0]
    bm = MOE_BLOCK_ROWS
    n_pairs = t_n * TOP_K
    n_blocks = n_pairs // bm + N_EXPERTS
    flat_e = eidx.reshape(-1)
    counts = jnp.zeros((N_EXPERTS,), jnp.int32).at[flat_e].add(1)
    cnt_start = jnp.cumsum(counts) - counts
    pad_counts = (counts + bm - 1) // bm * bm
    pad_end = jnp.cumsum(pad_counts)
    pad_start = pad_end - pad_counts
    order = jnp.argsort(flat_e, stable=True).astype(jnp.int32)
    se = flat_e[order]
    dest = pad_start[se] + jnp.arange(n_pairs, dtype=jnp.int32) - cnt_start[se]
    row_tok = jnp.zeros((n_blocks * bm,), jnp.int32).at[dest].set(order // TOP_K)
    pos = jnp.zeros((n_pairs,), jnp.int32).at[order].set(dest)
    blk_start = jnp.arange(n_blocks, dtype=jnp.int32) * bm
    blk_e = jnp.minimum(jnp.searchsorted(pad_end, blk_start, side="right"), N_EXPERTS - 1).astype(jnp.int32)
    blk_valid = (blk_start < pad_end[-1]).astype(jnp.int32)
    return row_tok, pos, blk_e, blk_valid


def moe_ffn(x, h, h_slabs, prm, layer, gate, n_prompt):
    t_n = h.shape[0]
    e8, w8 = moe_route(h, prm["router_w"][layer], prm["router_bias"][layer])
    row_tok, pos, blk_e, blk_valid = moe_dispatch_plan(e8[:TOP_K].T)
    xs = gather_rows(h_slabs, row_tok)
    ys = expert_ffn(xs, blk_e, blk_valid, prm["exp_w1"][layer].astype(BF), prm["exp_w3"][layer].astype(BF),
                    prm["exp_w2"][layer].astype(BF))
    yg = gather_rows(ys, pos.reshape(t_n, TOP_K).T.reshape(-1))
    return moe_combine(x, h, yg, w8.T, prm["sh_w1"][layer].astype(BF), prm["sh_w3"][layer].astype(BF),
                       prm["sh_w2"][layer].astype(BF), gate, n_prompt)


def _split_w_in(w):
    offs = np.concatenate([[0], np.cumsum(IN_SPLITS)])
    dn_qkv, dn_z, dn_b, dn_a, ssd_z, ssd_xbc, ssd_dt, hy_u, na_qkv, gates = [
        w[:, int(offs[i]):int(offs[i + 1])] for i in range(len(IN_SPLITS))]
    small = jnp.concatenate([dn_b, dn_a, ssd_dt], axis=1)
    small = jnp.pad(small, ((0, 0), (0, LANES - small.shape[1])))
    cast = lambda a: a.astype(BF)
    return dict(dn=cast(jnp.concatenate([dn_qkv, dn_z], axis=1)), ssd=cast(jnp.concatenate([ssd_xbc, ssd_z], axis=1)),
                hy=cast(hy_u), na=cast(na_qkv), gates=cast(gates), small=cast(small))


def kernel(x_prompt, x_sample, cache_k, cache_v, state_delta, state_ssd, c, c_ctx,
           ada_w, ada_b, norm_w, final_norm_w, w_in,
           dn_conv_w, dn_conv_b, dn_A_log, dn_dt_bias, dn_norm_w,
           ssd_conv_w, ssd_conv_b, ssd_A_log, ssd_dt_bias, ssd_D, ssd_norm_w,
           hy_conv_w, hy_conv_b, hy_f1_w, hy_f1_b, hy_f2_w, hy_f2_b, hy_f3_w, hy_freq, hy_decay, hy_bias,
           na_rpb, w_branch, w_out,
           router_w, router_bias, exp_w1, exp_w3, exp_w2, sh_w1, sh_w3, sh_w2):
    prm = dict(hy_conv_w=hy_conv_w, hy_conv_b=hy_conv_b, hy_f1_w=hy_f1_w, hy_f1_b=hy_f1_b, hy_f2_w=hy_f2_w,
               hy_f2_b=hy_f2_b, hy_f3_w=hy_f3_w, hy_freq=hy_freq, hy_decay=hy_decay, hy_bias=hy_bias,
               router_w=router_w, router_bias=router_bias, exp_w1=exp_w1, exp_w3=exp_w3, exp_w2=exp_w2,
               sh_w1=sh_w1, sh_w3=sh_w3, sh_w2=sh_w2)
    bp, sp, d = x_prompt.shape
    bs, ss, _ = x_sample.shape
    n_prompt = bp * sp
    x = jnp.concatenate([x_prompt.reshape(n_prompt, d), x_sample.reshape(bs * ss, d)], axis=0)
    t_n = x.shape[0]
    cvecs = jnp.zeros((8, d), F32).at[0].set(c_ctx).at[1:1 + bs].set(c)
    mod = ada_modulation(cvecs, ada_w, ada_b).reshape(DEPTH, 8, 6, 1, d)
    ck = cache_k.reshape(bs, DEPTH, cache_k.shape[2], NA_W)
    cv = cache_v.reshape(bs, DEPTH, cache_v.shape[2], NA_W)
    sd0 = state_delta
    ss0 = state_ssd.reshape(bs, DEPTH, 2, SSD_XW, SSD_N)
    passes = ((sp, 0, bp, True), (ss, n_prompt, bs, False))
    new_k, new_v, new_dn, new_ssd = [], [], [], []
    for l in range(DEPTH):
        sh1, sc1, g1, sh2, sc2, g2 = [mod[l, :, n] for n in range(6)]
        h = norm_modulate(x, norm_w[l, 0], sc1, sh1, n_prompt, BF)
        w = _split_w_in(w_in[l])
        dn_proj = matmul(h, w["dn"], BF)
        ssd_proj = matmul(h, w["ssd"], BF)
        hy_proj = matmul(h, w["hy"], BF)
        na_proj = matmul(h, w["na"], BF)
        gate_proj = matmul(h, w["gates"], BF, tn=1024)
        pcol = matmul(h, w["small"], F32)
        prow = pcol.reshape(t_n // CHUNK, CHUNK, LANES).transpose(0, 2, 1)
        pc_par, pr_par = gate_params(dn_A_log[l], dn_dt_bias[l], ssd_A_log[l], ssd_dt_bias[l])
        bias_tab = neighbourhood_bias_table(na_rpb[l])
        o_a, o_b, o_c, o_d = [], [], [], []
        for seq, row0, batch, is_ctx in passes:
            qkvc = conv_silu(dn_proj, dn_conv_w[l], dn_conv_b[l], DN_QKV, seq, row0, batch)
            of, ob, dn_fin = deltanet_scan(qkvc, pcol, prow, pc_par, pr_par, None if is_ctx else sd0,
                                           l, seq, row0, batch)
            o_a.append(deltanet_out(of, ob, dn_proj, dn_norm_w[l], row0))
            xbcc = conv_silu(ssd_proj, ssd_conv_w[l], ssd_conv_b[l], SSD_XBC, seq, row0, batch)
            yf, yb, ssd_fin = ssd_scan(xbcc, pcol, prow, pc_par, pr_par, None if is_ctx else ss0,
                                       l, seq, row0, batch)
            o_b.append(ssd_out(yf, yb, xbcc, ssd_proj, ssd_D[l], ssd_norm_w[l], row0))
            o_c.append(hyena_mixer(hy_proj, prm, l, seq, row0, batch))
            if is_ctx:
                o_d.append(context_attention(na_proj, seq, batch))
                new_k.append(na_proj[:n_prompt, NA_W:2 * NA_W].astype(F32).reshape(bp, sp, NA_HEADS, NA_HD))
                new_v.append(na_proj[:n_prompt, 2 * NA_W:].astype(F32).reshape(bp, sp, NA_HEADS, NA_HD))
                new_dn.append(dn_fin)
                new_ssd.append(ssd_fin.reshape(bp, 2, SSD_HEADS, SSD_P, SSD_N))
            else:
                o_d.append(neighbourhood_attention(na_proj, ck, cv, bias_tab, l, seq, row0, batch))
        branches = [jnp.concatenate(o, axis=0) for o in (o_a, o_b, o_c, o_d)]
        merged = branch_merge(branches, gate_proj, w_branch[l].astype(BF))
        x = matmul_residual(merged, w_out[l].astype(BF), x, g1, n_prompt)
        h2, h2_slabs = norm_modulate(x, norm_w[l, 1], sc2, sh2, n_prompt, F32, with_slabs=True)
        x = moe_ffn(x, h2, h2_slabs, prm, l, g2, n_prompt)
    y = final_norm(x, final_norm_w)
    return (y[:n_prompt].reshape(bp, sp, d), y[n_prompt:].reshape(bs, ss, d),
            jnp.stack(new_k, axis=1), jnp.stack(new_v, axis=1), jnp.stack(new_dn, axis=1), jnp.stack(new_ssd, axis=1))
```

```python
import functools
import math

import jax
import jax.numpy as jnp
import numpy as np
from jax import lax
from jax.experimental import pallas as pl
from jax.experimental.pallas import tpu as pltpu

BF = jnp.bfloat16
F32 = jnp.float32

D_MODEL = 2048
DEPTH = 2
DEC_SEQ = 4096
GRID_W = 64
EPS = 1e-6
NEG_INF = -1e30
N_BRANCH = 4
DN_HEADS = 4
DN_DK = 128
DN_DV = 128
CHUNK = 64
SSD_HEADS = 8
SSD_P = 64
SSD_GROUPS = 2
SSD_N = 128
HY_WIDTH = 512
HY_BANDS = 16
NA_HEADS = 8
NA_HD = 64
NA_WIN_R = 8
NA_WIN_C = 16
BRANCH_W = 512
N_EXPERTS = 64
TOP_K = 6
N_EXP_GROUPS = 8
TOPK_GROUPS = 4
EXPERT_FF = 512
ROUTED_SCALE = 2.5

DN_QKV = 2 * DN_HEADS * DN_DK + DN_HEADS * DN_DV
SSD_XBC = SSD_HEADS * SSD_P + 2 * SSD_GROUPS * SSD_N
IN_SPLITS = (DN_QKV, DN_HEADS * DN_DV, 2 * DN_HEADS, 2 * DN_HEADS,
             SSD_HEADS * SSD_P, SSD_XBC, 2 * SSD_HEADS,
             3 * HY_WIDTH, 3 * NA_HEADS * NA_HD, N_BRANCH * D_MODEL)

VMEM_LIMIT_BYTES = 56 * 1024 * 1024
LANES = 128
COL_DN_B = 0
COL_DN_A = 2 * DN_HEADS
COL_SSD_DT = 4 * DN_HEADS
CHUNK_GROUP = 4
MOE_BLOCK_ROWS = 512


def _params(sem):
    return pltpu.CompilerParams(dimension_semantics=sem, vmem_limit_bytes=VMEM_LIMIT_BYTES)


def _sigmoid(x):
    return jax.nn.sigmoid(x)


def _silu(x):
    return x * jax.nn.sigmoid(x)


def _dot(a, b):
    return jnp.dot(a.astype(BF), b.astype(BF), preferred_element_type=F32)


def _dot_nt(a, b):
    return lax.dot_general(a.astype(BF), b.astype(BF), (((1,), (1,)), ((), ())), preferred_element_type=F32)


def _dot_tn(a, b):
    return lax.dot_general(a.astype(BF), b.astype(BF), (((0,), (0,)), ((), ())), preferred_element_type=F32)


def _dot_f32(a, b):
    return jnp.dot(a, b, preferred_element_type=F32, precision=lax.Precision.HIGHEST)


def _mod_kernel(c_ref, w_ref, b_ref, o_ref):
    a = _silu(c_ref[...])
    o_ref[...] = _dot(a, w_ref[...]) + b_ref[...]


def ada_modulation(cvecs, ada_w, ada_b):
    n = ada_w.shape[-1]
    tn = 1024
    return pl.pallas_call(
        _mod_kernel,
        out_shape=jax.ShapeDtypeStruct((DEPTH, 8, n), F32),
        grid=(DEPTH, n // tn),
        in_specs=[pl.BlockSpec((8, D_MODEL), lambda l, j: (0, 0)),
                  pl.BlockSpec((None, D_MODEL, tn), lambda l, j: (l, 0, j)),
                  pl.BlockSpec((None, 1, tn), lambda l, j: (l, 0, j))],
        out_specs=pl.BlockSpec((None, 8, tn), lambda l, j: (l, 0, j)),
        compiler_params=_params(("parallel", "parallel")),
        name="ada_modulation",
    )(cvecs, ada_w, ada_b.reshape(DEPTH, 1, n))


def _mod_row(i, tm, n_prompt):
    start = i * tm
    return jnp.where(start < n_prompt, 0, 1 + (start - n_prompt) // DEC_SEQ)


def _norm_mod_kernel(x_ref, w_ref, sc_ref, sh_ref, o_ref):
    x = x_ref[...]
    y = x * lax.rsqrt(jnp.mean(x * x, axis=-1, keepdims=True) + EPS)
    o_ref[...] = ((y * w_ref[...]) * (1.0 + sc_ref[...]) + sh_ref[...]).astype(o_ref.dtype)


def _norm_kernel(x_ref, w_ref, o_ref):
    x = x_ref[...]
    y = x * lax.rsqrt(jnp.mean(x * x, axis=-1, keepdims=True) + EPS)
    o_ref[...] = (y * w_ref[...]).astype(o_ref.dtype)


ROW_SLABS = D_MODEL // LANES


def _to_slabs(ref, val):
    for s in range(ROW_SLABS):
        ref[:, s, :] = val[:, s * LANES:(s + 1) * LANES]


def _from_slabs(ref):
    return jnp.concatenate([ref[:, s, :] for s in range(ROW_SLABS)], axis=-1)


def _norm_mod_slab_kernel(x_ref, w_ref, sc_ref, sh_ref, o_ref, slab_ref):
    x = x_ref[...]
    y = x * lax.rsqrt(jnp.mean(x * x, axis=-1, keepdims=True) + EPS)
    h = (y * w_ref[...]) * (1.0 + sc_ref[...]) + sh_ref[...]
    o_ref[...] = h
    _to_slabs(slab_ref, h)


def norm_modulate(x, w, scale, shift, n_prompt, out_dtype, with_slabs=False):
    t = x.shape[0]
    tm = 512
    row = functools.partial(_mod_row, tm=tm, n_prompt=n_prompt)
    out_shape = jax.ShapeDtypeStruct((t, D_MODEL), out_dtype)
    out_specs = pl.BlockSpec((tm, D_MODEL), lambda i: (i, 0))
    if with_slabs:
        out_shape = (out_shape, jax.ShapeDtypeStruct((t, ROW_SLABS, LANES), F32))
        out_specs = (out_specs, pl.BlockSpec((tm, ROW_SLABS, LANES), lambda i: (i, 0, 0)))
    return pl.pallas_call(
        _norm_mod_slab_kernel if with_slabs else _norm_mod_kernel,
        out_shape=out_shape,
        grid=(t // tm,),
        in_specs=[pl.BlockSpec((tm, D_MODEL), lambda i: (i, 0)),
                  pl.BlockSpec((1, D_MODEL), lambda i: (0, 0)),
                  pl.BlockSpec((None, 1, D_MODEL), lambda i: (row(i), 0, 0)),
                  pl.BlockSpec((None, 1, D_MODEL), lambda i: (row(i), 0, 0))],
        out_specs=out_specs,
        compiler_params=_params(("parallel",)),
        name="norm_modulate",
    )(x, w.reshape(1, D_MODEL), scale, shift)


def final_norm(x, w):
    t = x.shape[0]
    tm = 512
    return pl.pallas_call(
        _norm_kernel,
        out_shape=jax.ShapeDtypeStruct((t, D_MODEL), F32),
        grid=(t // tm,),
        in_specs=[pl.BlockSpec((tm, D_MODEL), lambda i: (i, 0)),
                  pl.BlockSpec((1, D_MODEL), lambda i: (0, 0))],
        out_specs=pl.BlockSpec((tm, D_MODEL), lambda i: (i, 0)),
        compiler_params=_params(("parallel",)),
        name="final_norm",
    )(x, w.reshape(1, D_MODEL))


def _mm_kernel(a_ref, w_ref, o_ref):
    o_ref[...] = jnp.dot(a_ref[...], w_ref[...], preferred_element_type=F32).astype(o_ref.dtype)


def _mm_residual_kernel(a_ref, w_ref, x_ref, g_ref, o_ref):
    acc = jnp.dot(a_ref[...], w_ref[...], preferred_element_type=F32)
    o_ref[...] = x_ref[...] + g_ref[...] * acc


def matmul(a, w, out_dtype, tm=1024, tn=512):
    m, k = a.shape
    n = w.shape[1]
    tn = min(tn, n)
    return pl.pallas_call(
        _mm_kernel,
        out_shape=jax.ShapeDtypeStruct((m, n), out_dtype),
        grid=(m // tm, n // tn),
        in_specs=[pl.BlockSpec((tm, k), lambda i, j: (i, 0)),
                  pl.BlockSpec((k, tn), lambda i, j: (0, j))],
        out_specs=pl.BlockSpec((tm, tn), lambda i, j: (i, j)),
        compiler_params=_params(("parallel", "arbitrary")),
        name="matmul",
    )(a, w)


def matmul_residual(a, w, x, gate, n_prompt, tm=1024, tn=512):
    m, k = a.shape
    n = w.shape[1]
    row = functools.partial(_mod_row, tm=tm, n_prompt=n_prompt)
    return pl.pallas_call(
        _mm_residual_kernel,
        out_shape=jax.ShapeDtypeStruct((m, n), F32),
        grid=(m // tm, n // tn),
        in_specs=[pl.BlockSpec((tm, k), lambda i, j: (i, 0)),
                  pl.BlockSpec((k, tn), lambda i, j: (0, j)),
                  pl.BlockSpec((tm, tn), lambda i, j: (i, j)),
                  pl.BlockSpec((None, 1, tn), lambda i, j: (row(i), 0, j))],
        out_specs=pl.BlockSpec((tm, tn), lambda i, j: (i, j)),
        compiler_params=_params(("parallel", "arbitrary")),
        name="matmul_residual",
    )(a, w, x, gate)


def _conv3(x, w, b):
    length = x.shape[0]
    row = lax.broadcasted_iota(jnp.int32, x.shape, 0)
    prev = jnp.where(row == 0, 0.0, pltpu.roll(x, 1, 0))
    nxt = jnp.where(row == length - 1, 0.0, pltpu.roll(x, length - 1, 0))
    return prev * w[0:1] + x * w[1:2] + nxt * w[2:3] + b


def _conv_silu_kernel(u_ref, w_ref, b_ref, o_ref):
    y = _conv3(u_ref[...].astype(F32), w_ref[...], b_ref[...])
    o_ref[...] = _silu(y).astype(o_ref.dtype)


def conv_silu(proj, conv_w, conv_b, width, seq, row0, batch):
    ct = 128 if seq > 1024 else 512
    rb0 = row0 // seq
    return pl.pallas_call(
        _conv_silu_kernel,
        out_shape=jax.ShapeDtypeStruct((batch * seq, width), BF),
        grid=(batch, width // ct),
        in_specs=[pl.BlockSpec((seq, ct), lambda b, j: (rb0 + b, j)),
                  pl.BlockSpec((3, ct), lambda b, j: (0, j)),
                  pl.BlockSpec((1, ct), lambda b, j: (0, j))],
        out_specs=pl.BlockSpec((seq, ct), lambda b, j: (b, j)),
        compiler_params=_params(("parallel", "parallel")),
        name="conv_silu",
    )(proj, conv_w, conv_b.reshape(1, width))


def _chunk_masks():
    ii = lax.broadcasted_iota(jnp.int32, (CHUNK, CHUNK), 0)
    jj = lax.broadcasted_iota(jnp.int32, (CHUNK, CHUNK), 1)
    return ii, jj


def _log_decay(p, alog, dtb):
    return -jnp.exp(alog) * jax.nn.softplus(p + dtb)


def _cumulative(gcol, grow, lm, um, reverse):
    if not reverse:
        return _dot_f32(lm, gcol), _dot_f32(grow, um)
    return _dot_f32(um, gcol), _dot_f32(grow, lm)


def _inv_unit_triangular(a, eye):
    p = eye - a
    ak = a
    for _ in range(5):
        ak = _dot_f32(ak, ak)
        p = p + _dot_f32(p, ak)
    return p


def _dn_scan_kernel(*refs, zero_init):
    if zero_init:
        (qf, kf, vf, pcf, prf, qb, kb_, vb, pcb, prb, pc_par, pr_par, of, ob, sref) = refs
        s0 = None
    else:
        (qf, kf, vf, pcf, prf, qb, kb_, vb, pcb, prb, pc_par, pr_par, s0, of, ob, sref) = refs
    g = pl.program_id(1)

    @pl.when(g == 0)
    def _():
        if zero_init:
            sref[...] = jnp.zeros(sref.shape, F32)
        else:
            sref[...] = s0[...]

    ii, jj = _chunk_masks()
    eye = (ii == jj).astype(F32)
    lm = (ii >= jj).astype(F32)
    um = (ii <= jj).astype(F32)
    alog_c = pc_par[0:1, :]
    dtb_c = pc_par[1:2, :]
    alog_r = pr_par[:, 0:1]
    dtb_r = pr_par[:, 1:2]
    dirs = ((qf, kf, vf, pcf, prf, of), (qb, kb_, vb, pcb, prb, ob))

    def chunk_step(c, carry):
        for d in range(2):
            q_ref, k_ref, v_ref, pc_ref, pr_ref, o_ref = dirs[d]
            cc = c if d == 0 else CHUNK_GROUP - 1 - c
            r0 = pl.multiple_of(cc * CHUNK, CHUNK)
            pc = pc_ref[pl.ds(r0, CHUNK), :]
            pr = pr_ref[cc]
            gc_col, gc_row = _cumulative(_log_decay(pc, alog_c, dtb_c), _log_decay(pr, alog_r, dtb_r),
                                         lm, um, reverse=(d == 1))
            beta_all = _sigmoid(pc)
            incl = (ii >= jj) if d == 0 else (ii <= jj)
            strict = (ii > jj) if d == 0 else (ii < jj)
            last = CHUNK - 1 if d == 0 else 0
            for h in range(DN_HEADS):
                ci = d * DN_HEADS + h
                sl = slice(h * DN_DK, (h + 1) * DN_DK)
                q = q_ref[pl.ds(r0, CHUNK), sl].astype(F32)
                k = k_ref[pl.ds(r0, CHUNK), sl].astype(F32)
                v = v_ref[pl.ds(r0, CHUNK), sl].astype(F32)
                q = q * lax.rsqrt(jnp.sum(q * q, axis=-1, keepdims=True) + EPS) * (DN_DK ** -0.5)
                k = k * lax.rsqrt(jnp.sum(k * k, axis=-1, keepdims=True) + EPS)
                gcc = gc_col[:, COL_DN_A + ci:COL_DN_A + ci + 1]
                gcr = gc_row[COL_DN_A + ci:COL_DN_A + ci + 1, :]
                glast = gc_col[last:last + 1, COL_DN_A + ci:COL_DN_A + ci + 1]
                decay = jnp.exp(jnp.where(incl, gcc - gcr, -jnp.inf))
                beta = beta_all[:, COL_DN_B + ci:COL_DN_B + ci + 1]
                kbeta = k * beta
                a_mat = jnp.where(strict, _dot_nt(kbeta, k) * decay, 0.0)
                t_mat = _inv_unit_triangular(a_mat, eye)
                egc = jnp.exp(gcc)
                sol = _dot_f32(t_mat, jnp.concatenate([v * beta, kbeta * egc], axis=-1))
                u = sol[:, :DN_DV]
                w = sol[:, DN_DV:]
                s = sref[d, h]
                v_new = u - _dot(w, s)
                attn = _dot_nt(q, k) * decay
                o_ref[pl.ds(r0, CHUNK), sl] = _dot(q * egc, s) + _dot(attn, v_new)
                sref[d, h] = s * jnp.exp(glast) + _dot_tn(k * jnp.exp(glast - gcc), v_new)
        return carry

    lax.fori_loop(0, CHUNK_GROUP, chunk_step, 0)


def deltanet_scan(qkv, pcol, prow, pc_par, pr_par, s0, layer, seq, row0, batch):
    gt = CHUNK_GROUP * CHUNK
    ncg = seq // gt
    hw = DN_HEADS * DN_DK
    pc0 = row0 // gt
    zero_init = s0 is None

    def fwd(b, g):
        return b * ncg + g

    def bwd(b, g):
        return b * ncg + ncg - 1 - g

    def qkv_specs(pos):
        return [pl.BlockSpec((gt, hw), lambda b, g, n=n: (pos(b, g), n)) for n in range(3)]

    def gate_specs(pos):
        return [pl.BlockSpec((gt, LANES), lambda b, g: (pc0 + pos(b, g), 0)),
                pl.BlockSpec((CHUNK_GROUP, LANES, CHUNK), lambda b, g: (pc0 + pos(b, g), 0, 0))]

    in_specs = (qkv_specs(fwd) + gate_specs(fwd) + qkv_specs(bwd) + gate_specs(bwd)
                + [pl.BlockSpec((8, LANES), lambda b, g: (0, 0)),
                   pl.BlockSpec((LANES, 8), lambda b, g: (0, 0))])
    args = [qkv, qkv, qkv, pcol, prow, qkv, qkv, qkv, pcol, prow, pc_par, pr_par]
    if not zero_init:
        in_specs.append(pl.BlockSpec((None, None, 2, DN_HEADS, DN_DK, DN_DV),
                                     lambda b, g: (b, layer, 0, 0, 0, 0)))
        args.append(s0)
    return pl.pallas_call(
        functools.partial(_dn_scan_kernel, zero_init=zero_init),
        out_shape=(jax.ShapeDtypeStruct((batch * seq, hw), F32),
                   jax.ShapeDtypeStruct((batch * seq, hw), F32),
                   jax.ShapeDtypeStruct((batch, 2, DN_HEADS, DN_DK, DN_DV), F32)),
        grid=(batch, ncg),
        in_specs=in_specs,
        out_specs=(pl.BlockSpec((gt, hw), lambda b, g: (fwd(b, g), 0)),
                   pl.BlockSpec((gt, hw), lambda b, g: (bwd(b, g), 0)),
                   pl.BlockSpec((None, 2, DN_HEADS, DN_DK, DN_DV), lambda b, g: (b, 0, 0, 0, 0))),
        compiler_params=_params(("parallel", "arbitrary")),
        name="deltanet_scan",
    )(*args)


def _dn_out_kernel(of_ref, ob_ref, z_ref, w_ref, o_ref):
    for h in range(DN_HEADS):
        sl = slice(h * DN_DV, (h + 1) * DN_DV)
        s = of_ref[:, sl] + ob_ref[:, sl]
        y = s * lax.rsqrt(jnp.mean(s * s, axis=-1, keepdims=True) + EPS) * w_ref[...]
        o_ref[:, sl] = (y * _silu(z_ref[:, sl].astype(F32))).astype(o_ref.dtype)


def deltanet_out(o_f, o_b, proj, norm_w, row0):
    n = o_f.shape[0]
    tm = 512
    hw = DN_HEADS * DN_DV
    return pl.pallas_call(
        _dn_out_kernel,
        out_shape=jax.ShapeDtypeStruct((n, hw), BF),
        grid=(n // tm,),
        in_specs=[pl.BlockSpec((tm, hw), lambda i: (i, 0)),
                  pl.BlockSpec((tm, hw), lambda i: (i, 0)),
                  pl.BlockSpec((tm, hw), lambda i: (row0 // tm + i, DN_QKV // hw)),
                  pl.BlockSpec((1, DN_DV), lambda i: (0, 0))],
        out_specs=pl.BlockSpec((tm, hw), lambda i: (i, 0)),
        compiler_params=_params(("parallel",)),
        name="deltanet_out",
    )(o_f, o_b, proj, norm_w.reshape(1, DN_DV))


def gate_params(dn_a_log, dn_dt_bias, ssd_a_log, ssd_dt_bias):
    def lay(dn, ssd):
        v = jnp.zeros((LANES,), F32)
        v = v.at[COL_DN_A:COL_DN_A + 2 * DN_HEADS].set(dn.reshape(-1).astype(F32))
        return v.at[COL_SSD_DT:COL_SSD_DT + 2 * SSD_HEADS].set(ssd.reshape(-1).astype(F32))
    par = jnp.zeros((8, LANES), F32).at[0].set(lay(dn_a_log, ssd_a_log)).at[1].set(lay(dn_dt_bias, ssd_dt_bias))
    return par, par.T


SSD_XW = SSD_HEADS * SSD_P
SSD_GW = SSD_GROUPS * SSD_N
HEADS_PER_GROUP = SSD_HEADS // SSD_GROUPS


def _ssd_scan_kernel(*refs, zero_init):
    if zero_init:
        (xf, pcf, prf, xb, pcb, prb, pc_par, pr_par, of, ob, sref) = refs
        s0 = None
    else:
        (xf, pcf, prf, xb, pcb, prb, pc_par, pr_par, s0, of, ob, sref) = refs
    g = pl.program_id(1)

    @pl.when(g == 0)
    def _():
        if zero_init:
            sref[...] = jnp.zeros(sref.shape, F32)
        else:
            sref[...] = s0[...]

    ii, jj = _chunk_masks()
    lm = (ii >= jj).astype(F32)
    um = (ii <= jj).astype(F32)
    alog_c = pc_par[0:1, :]
    dtb_c = pc_par[1:2, :]
    alog_r = pr_par[:, 0:1]
    dtb_r = pr_par[:, 1:2]
    dirs = ((xf, pcf, prf, of), (xb, pcb, prb, ob))
    gw = HEADS_PER_GROUP * SSD_P

    def chunk_step(c, carry):
        for d in range(2):
            x_ref, pc_ref, pr_ref, o_ref = dirs[d]
            cc = c if d == 0 else CHUNK_GROUP - 1 - c
            r0 = pl.multiple_of(cc * CHUNK, CHUNK)
            rows = pl.ds(r0, CHUNK)
            pc = pc_ref[rows, :]
            pr = pr_ref[cc]
            dt_all = jax.nn.softplus(pc + dtb_c)
            ac_col, ac_row = _cumulative(_log_decay(pc, alog_c, dtb_c), _log_decay(pr, alog_r, dtb_r),
                                         lm, um, reverse=(d == 1))
            incl = (ii >= jj) if d == 0 else (ii <= jj)
            last = CHUNK - 1 if d == 0 else 0
            for gi in range(SSD_GROUPS):
                bm = x_ref[rows, SSD_XW + gi * SSD_N:SSD_XW + (gi + 1) * SSD_N]
                cm = x_ref[rows, SSD_XW + SSD_GW + gi * SSD_N:SSD_XW + SSD_GW + (gi + 1) * SSD_N]
                scores = _dot_nt(cm, bm)
                srow = pl.ds(gi * gw, gw)
                s_grp = sref[d, srow, :]
                y_state = _dot_nt(cm, s_grp)
                ys, xs, decs = [], [], []
                for hh in range(HEADS_PER_GROUP):
                    h = gi * HEADS_PER_GROUP + hh
                    col = COL_SSD_DT + d * SSD_HEADS + h
                    acc = ac_col[:, col:col + 1]
                    acr = ac_row[col:col + 1, :]
                    alast = ac_col[last:last + 1, col:col + 1]
                    decay = jnp.exp(jnp.where(incl, acc - acr, -jnp.inf))
                    xdt = x_ref[rows, h * SSD_P:(h + 1) * SSD_P].astype(F32) * dt_all[:, col:col + 1]
                    ys.append(_dot(scores * decay, xdt) + y_state[:, hh * SSD_P:(hh + 1) * SSD_P] * jnp.exp(acc))
                    xs.append(xdt * jnp.exp(alast - acc))
                    decs.append(jnp.broadcast_to(jnp.exp(alast), (SSD_P, 1)))
                o_ref[rows, gi * gw:(gi + 1) * gw] = jnp.concatenate(ys, axis=-1)
                s_add = _dot_tn(jnp.concatenate(xs, axis=-1), bm)
                sref[d, srow, :] = s_grp * jnp.concatenate(decs, axis=0) + s_add
        return carry

    lax.fori_loop(0, CHUNK_GROUP, chunk_step, 0)


def ssd_scan(xbc, pcol, prow, pc_par, pr_par, s0, layer, seq, row0, batch):
    gt = CHUNK_GROUP * CHUNK
    ncg = seq // gt
    pc0 = row0 // gt
    zero_init = s0 is None

    def fwd(b, g):
        return b * ncg + g

    def bwd(b, g):
        return b * ncg + ncg - 1 - g

    def specs(pos):
        return [pl.BlockSpec((gt, SSD_XBC), lambda b, g: (pos(b, g), 0)),
                pl.BlockSpec((gt, LANES), lambda b, g: (pc0 + pos(b, g), 0)),
                pl.BlockSpec((CHUNK_GROUP, LANES, CHUNK), lambda b, g: (pc0 + pos(b, g), 0, 0))]

    in_specs = specs(fwd) + specs(bwd) + [pl.BlockSpec((8, LANES), lambda b, g: (0, 0)),
                                          pl.BlockSpec((LANES, 8), lambda b, g: (0, 0))]
    args = [xbc, pcol, prow, xbc, pcol, prow, pc_par, pr_par]
    if not zero_init:
        in_specs.append(pl.BlockSpec((None, None, 2, SSD_XW, SSD_N), lambda b, g: (b, layer, 0, 0, 0)))
        args.append(s0)
    return pl.pallas_call(
        functools.partial(_ssd_scan_kernel, zero_init=zero_init),
        out_shape=(jax.ShapeDtypeStruct((batch * seq, SSD_XW), F32),
                   jax.ShapeDtypeStruct((batch * seq, SSD_XW), F32),
                   jax.ShapeDtypeStruct((batch, 2, SSD_XW, SSD_N), F32)),
        grid=(batch, ncg),
        in_specs=in_specs,
        out_specs=(pl.BlockSpec((gt, SSD_XW), lambda b, g: (fwd(b, g), 0)),
                   pl.BlockSpec((gt, SSD_XW), lambda b, g: (bwd(b, g), 0)),
                   pl.BlockSpec((None, 2, SSD_XW, SSD_N), lambda b, g: (b, 0, 0, 0))),
        compiler_params=_params(("parallel", "arbitrary")),
        name="ssd_scan",
    )(*args)


def _ssd_out_kernel(yf_ref, yb_ref, x_ref, z_ref, d_ref, w_ref, o_ref):
    y = yf_ref[...] + yb_ref[...] + x_ref[...].astype(F32) * d_ref[...]
    y = y * _silu(z_ref[...].astype(F32))
    o_ref[...] = (y * lax.rsqrt(jnp.mean(y * y, axis=-1, keepdims=True) + EPS) * w_ref[...]).astype(o_ref.dtype)


def ssd_out(y_f, y_b, xbc, proj, d_skip, norm_w, row0):
    n = y_f.shape[0]
    tm = 512
    return pl.pallas_call(
        _ssd_out_kernel,
        out_shape=jax.ShapeDtypeStruct((n, SSD_XW), BF),
        grid=(n // tm,),
        in_specs=[pl.BlockSpec((tm, SSD_XW), lambda i: (i, 0)),
                  pl.BlockSpec((tm, SSD_XW), lambda i: (i, 0)),
                  pl.BlockSpec((tm, SSD_XW), lambda i: (i, 0)),
                  pl.BlockSpec((tm, SSD_XW), lambda i: (row0 // tm + i, SSD_XBC // SSD_XW)),
                  pl.BlockSpec((1, SSD_XW), lambda i: (0, 0)),
                  pl.BlockSpec((1, SSD_XW), lambda i: (0, 0))],
        out_specs=pl.BlockSpec((tm, SSD_XW), lambda i: (i, 0)),
        compiler_params=_params(("parallel",)),
        name="ssd_out",
    )(y_f, y_b, xbc, proj, jnp.repeat(d_skip.astype(F32), SSD_P).reshape(1, SSD_XW), norm_w.reshape(1, SSD_XW))


HY_FEAT_PAD = 128
HY_HID = 64


def _hy_filter_kernel(feat_ref, f1w, f1b, f2w, f2b, f3w, fq, dec, o_ref):
    feat = feat_ref[...]
    hdn = jnp.sin(fq[...] * (_dot_f32(feat, f1w[...]) + f1b[...]))
    hdn = jnp.sin(fq[...] * (_dot_f32(hdn, f2w[...]) + f2b[...]))
    filt = _dot_f32(hdn, f3w[...])
    window = jnp.exp(-jnp.abs(dec[...]) * (2.0 * jnp.abs(feat[:, 0:1])))
    o_ref[...] = filt * window


def hyena_filter(seq, f1_w, f1_b, f2_w, f2_b, f3_w, freq, decay):
    off = (jnp.arange(seq, dtype=F32) - (seq // 2)) / seq
    ang = 2.0 * math.pi * off[:, None] * jnp.arange(1, HY_BANDS + 1, dtype=F32)
    feat = jnp.concatenate([off[:, None], jnp.sin(ang), jnp.cos(ang)], axis=-1)
    emb = feat.shape[1]
    feat = jnp.pad(feat, ((0, 0), (0, HY_FEAT_PAD - emb)))
    f1p = jnp.pad(f1_w.astype(F32), ((0, HY_FEAT_PAD - emb), (0, 0)))
    tl = min(seq, 512)
    full = lambda shape: pl.BlockSpec(shape, lambda i: (0, 0))
    return pl.pallas_call(
        _hy_filter_kernel,
        out_shape=jax.ShapeDtypeStruct((seq, HY_WIDTH), F32),
        grid=(seq // tl,),
        in_specs=[pl.BlockSpec((tl, HY_FEAT_PAD), lambda i: (i, 0)),
                  full((HY_FEAT_PAD, HY_HID)), full((1, HY_HID)), full((HY_HID, HY_HID)), full((1, HY_HID)),
                  full((HY_HID, HY_WIDTH)), full((1, HY_HID)), full((1, HY_WIDTH))],
        out_specs=pl.BlockSpec((tl, HY_WIDTH), lambda i: (i, 0)),
        compiler_params=_params(("parallel",)),
        name="hyena_filter",
    )(feat, f1p, f1_b.reshape(1, HY_HID), f2_w, f2_b.reshape(1, HY_HID), f3_w,
      freq.reshape(1, HY_HID), decay.reshape(1, HY_WIDTH))


def _hy_pre_kernel(u0, u1, u2, w0, w1, w2, b0, b1, b2, x0_ref, vv_ref):
    x0 = _conv3(u0[...].astype(F32), w0[...], b0[...])
    x1 = _conv3(u1[...].astype(F32), w1[...], b1[...])
    v = _conv3(u2[...].astype(F32), w2[...], b2[...])
    x0_ref[...] = x0.astype(x0_ref.dtype)
    vv_ref[...] = (v * x1).astype(vv_ref.dtype)


def hyena_pre(proj, conv_w, conv_b, seq, row0, batch):
    ct = 128
    nct = HY_WIDTH // ct
    rb0 = row0 // seq
    cb = conv_b.reshape(1, 3 * HY_WIDTH)
    u_specs = [pl.BlockSpec((seq, ct), lambda b, j, n=n: (rb0 + b, n * nct + j)) for n in range(3)]
    w_specs = [pl.BlockSpec((3, ct), lambda b, j, n=n: (0, n * nct + j)) for n in range(3)]
    b_specs = [pl.BlockSpec((1, ct), lambda b, j, n=n: (0, n * nct + j)) for n in range(3)]
    return pl.pallas_call(
        _hy_pre_kernel,
        out_shape=(jax.ShapeDtypeStruct((batch * seq, HY_WIDTH), BF),
                   jax.ShapeDtypeStruct((batch, seq, HY_WIDTH), BF)),
        grid=(batch, nct),
        in_specs=u_specs + w_specs + b_specs,
        out_specs=(pl.BlockSpec((seq, ct), lambda b, j: (b, j)),
                   pl.BlockSpec((None, seq, ct), lambda b, j: (b, 0, j))),
        compiler_params=_params(("parallel", "parallel")),
        name="hyena_pre",
    )(proj, proj, proj, conv_w, conv_w, conv_w, cb, cb, cb)


def _trig_tables(seq):
    n = 2 * seq
    nblk = seq // LANES

    def trig(m):
        ang = (m % (2 * n)).astype(F32) * (math.pi / n)
        return jnp.cos(ang), jnp.sin(ang)

    k = jnp.arange(seq, dtype=jnp.int32)[:, None]
    blk = jnp.arange(nblk, dtype=jnp.int32)[None, :]
    lane = jnp.arange(LANES, dtype=jnp.int32)[None, :]
    ca, sa = trig((2 * k + 1) * (LANES * blk))
    fwd_c0 = jnp.concatenate([ca, -sa], axis=0)
    fwd_s0 = jnp.concatenate([sa, ca], axis=0)
    fwd_cd, fwd_sd = trig((2 * k + 1) * lane)
    tp = k + seq // 2
    ca, sa = trig((2 * LANES * blk + 1) * tp)
    scale = 2.0 / n
    inv_c0 = jnp.concatenate([ca, -sa], axis=1) * scale
    inv_s0 = jnp.concatenate([sa, ca], axis=1) * scale
    inv_cd, inv_sd = trig(2 * lane * tp)
    return (fwd_c0, fwd_s0, fwd_cd, fwd_sd), (inv_c0, inv_s0, inv_cd, inv_sd)


def _trig_tile(dst, r0, rows, c0, s0, cd, sd):
    for j in range(c0.shape[1]):
        dst[r0:r0 + rows, j * LANES:(j + 1) * LANES] = (c0[:, j:j + 1] * cd - s0[:, j:j + 1] * sd).astype(dst.dtype)


def _dft_fwd_kernel(*refs, mul_h, tk):
    if mul_h:
        c0a, s0a, c0b, s0b, cd, sd, x_ref, h_ref, o_ref, lhs = refs
    else:
        c0a, s0a, c0b, s0b, cd, sd, x_ref, o_ref, lhs = refs

    @pl.when(pl.program_id(1) == 0)
    def _():
        _trig_tile(lhs, 0, tk, c0a[...], s0a[...], cd[...], sd[...])
        _trig_tile(lhs, tk, tk, c0b[...], s0b[...], cd[...], sd[...])

    acc = jnp.dot(lhs[...], x_ref[...].astype(BF), preferred_element_type=F32)
    ure = acc[:tk]
    uim = acc[tk:]
    if mul_h:
        hre = h_ref[0]
        him = h_ref[1]
        o_ref[0] = (ure * hre - uim * him).astype(o_ref.dtype)
        o_ref[1] = (ure * him + uim * hre).astype(o_ref.dtype)
    else:
        o_ref[0] = ure.astype(o_ref.dtype)
        o_ref[1] = uim.astype(o_ref.dtype)


def dft_forward(x, tables, h, out_dtype):
    batch, seq, width = x.shape
    c0, s0, cd, sd = tables
    nblk = seq // LANES
    tk = min(256, seq)
    nk = seq // tk
    mul_h = h is not None
    in_specs = [pl.BlockSpec((tk, nblk), lambda i, b: (i, 0)),
                pl.BlockSpec((tk, nblk), lambda i, b: (i, 0)),
                pl.BlockSpec((tk, nblk), lambda i, b: (nk + i, 0)),
                pl.BlockSpec((tk, nblk), lambda i, b: (nk + i, 0)),
                pl.BlockSpec((tk, LANES), lambda i, b: (i, 0)),
                pl.BlockSpec((tk, LANES), lambda i, b: (i, 0)),
                pl.BlockSpec((None, seq, width), lambda i, b: (b, 0, 0))]
    args = [c0, s0, c0, s0, cd, sd, x]
    if mul_h:
        in_specs.append(pl.BlockSpec((None, 2, tk, width), lambda i, b: (0, 0, i, 0)))
        args.append(h)
    return pl.pallas_call(
        functools.partial(_dft_fwd_kernel, mul_h=mul_h, tk=tk),
        out_shape=jax.ShapeDtypeStruct((batch, 2, seq, width), out_dtype),
        grid=(nk, batch),
        in_specs=in_specs,
        out_specs=pl.BlockSpec((None, 2, tk, width), lambda i, b: (b, 0, i, 0)),
        scratch_shapes=[pltpu.VMEM((2 * tk, seq), BF)],
        compiler_params=_params(("parallel", "arbitrary")),
        name="dft_forward",
    )(*args)


def _dft_inv_kernel(c0, s0, cd, sd, y_ref, vv_ref, x0_ref, bias_ref, o_ref, lhs):
    @pl.when(pl.program_id(1) == 0)
    def _():
        _trig_tile(lhs, 0, lhs.shape[0], c0[...], s0[...], cd[...], sd[...])

    conv = jnp.dot(lhs[...], y_ref[...], preferred_element_type=F32)
    vv = vv_ref[...].astype(F32)
    o_ref[...] = ((conv + vv * bias_ref[...]) * x0_ref[...].astype(F32)).astype(o_ref.dtype)


def dft_inverse_gate(y, tables, vv, x0, bias):
    batch, seq, width = vv.shape
    c0, s0, cd, sd = tables
    nblk2 = c0.shape[1]
    tm = min(256, seq)
    nt = seq // tm
    return pl.pallas_call(
        _dft_inv_kernel,
        out_shape=jax.ShapeDtypeStruct((batch * seq, width), BF),
        grid=(nt, batch),
        in_specs=[pl.BlockSpec((tm, nblk2), lambda i, b: (i, 0)),
                  pl.BlockSpec((tm, nblk2), lambda i, b: (i, 0)),
                  pl.BlockSpec((tm, LANES), lambda i, b: (i, 0)),
                  pl.BlockSpec((tm, LANES), lambda i, b: (i, 0)),
                  pl.BlockSpec((None, 2 * seq, width), lambda i, b: (b, 0, 0)),
                  pl.BlockSpec((None, tm, width), lambda i, b: (b, i, 0)),
                  pl.BlockSpec((tm, width), lambda i, b: (b * nt + i, 0)),
                  pl.BlockSpec((1, width), lambda i, b: (0, 0))],
        out_specs=pl.BlockSpec((tm, width), lambda i, b: (b * nt + i, 0)),
        scratch_shapes=[pltpu.VMEM((tm, 2 * seq), BF)],
        compiler_params=_params(("parallel", "arbitrary")),
        name="dft_inverse_gate",
    )(c0, s0, cd, sd, y, vv, x0, bias.reshape(1, width).astype(F32))


def hyena_mixer(proj, prm, layer, seq, row0, batch):
    filt = hyena_filter(seq, prm["hy_f1_w"][layer], prm["hy_f1_b"][layer], prm["hy_f2_w"][layer],
                        prm["hy_f2_b"][layer], prm["hy_f3_w"][layer], prm["hy_freq"][layer],
                        prm["hy_decay"][layer])
    fwd_tab, inv_tab = _trig_tables(seq)
    h_spec = dft_forward(filt[None], fwd_tab, None, F32)
    x0, vv = hyena_pre(proj, prm["hy_conv_w"][layer], prm["hy_conv_b"][layer], seq, row0, batch)
    y_spec = dft_forward(vv, fwd_tab, h_spec, BF)
    return dft_inverse_gate(y_spec.reshape(batch, 2 * seq, HY_WIDTH), inv_tab, vv, x0, prm["hy_bias"][layer])


NA_W = NA_HEADS * NA_HD
NA_SCALE = NA_HD ** -0.5


def _softmax_attend(scores, values):
    m = functools.reduce(jnp.maximum, [jnp.max(s, axis=-1, keepdims=True) for s in scores])
    ps = [jnp.exp(s - m) for s in scores]
    denom = functools.reduce(lambda a, b: a + b, [jnp.sum(p, axis=-1, keepdims=True) for p in ps])
    num = functools.reduce(lambda a, b: a + b, [_dot(p, v) for p, v in zip(ps, values)])
    return num / denom


def _ctx_attn_kernel(q_ref, k_ref, v_ref, o_ref):
    outs = []
    for h in range(NA_HEADS):
        sl = slice(h * NA_HD, (h + 1) * NA_HD)
        s = _dot_nt(q_ref[:, sl], k_ref[:, sl]) * NA_SCALE
        outs.append(_softmax_attend([s], [v_ref[:, sl]]))
    o_ref[...] = jnp.concatenate(outs, axis=-1).astype(o_ref.dtype)


def context_attention(proj, seq, batch):
    return pl.pallas_call(
        _ctx_attn_kernel,
        out_shape=jax.ShapeDtypeStruct((batch * seq, NA_W), BF),
        grid=(batch,),
        in_specs=[pl.BlockSpec((seq, NA_W), lambda b, n=n: (b, n)) for n in range(3)],
        out_specs=pl.BlockSpec((seq, NA_W), lambda b: (b, 0)),
        compiler_params=_params(("parallel",)),
        name="context_attention",
    )(proj, proj, proj)


def _na_row_start(r, rows_n):
    return jnp.clip(r - NA_WIN_R // 2, 0, rows_n - NA_WIN_R)


def _nbr_attn_kernel(q_ref, k_ref, v_ref, kc_ref, vc_ref, bias_ref, o_ref, *, rows_n):
    r = pl.program_id(1)
    koff = pl.multiple_of(_na_row_start(r, rows_n) * GRID_W, GRID_W)
    win = pl.ds(koff, NA_WIN_R * GRID_W)
    outs = []
    for h in range(NA_HEADS):
        sl = slice(h * NA_HD, (h + 1) * NA_HD)
        q = q_ref[:, sl]
        bias = bias_ref[h]
        s_loc = _dot_nt(q, k_ref[win, sl]) * NA_SCALE
        s_loc = jnp.where(bias > 0.5 * NEG_INF, s_loc + bias, NEG_INF)
        s_ctx = _dot_nt(q, kc_ref[:, sl]) * NA_SCALE
        outs.append(_softmax_attend([s_loc, s_ctx], [v_ref[win, sl], vc_ref[:, sl]]))
    o_ref[...] = jnp.concatenate(outs, axis=-1).astype(o_ref.dtype)


def neighbourhood_bias_table(rpb):
    cols = jnp.arange(GRID_W)
    col_start = jnp.clip(cols - NA_WIN_C // 2, 0, GRID_W - NA_WIN_C)
    col_in = (cols[None, :] >= col_start[:, None]) & (cols[None, :] < col_start[:, None] + NA_WIN_C)
    dc = jnp.clip(cols[None, :] - cols[:, None], -(NA_WIN_C - 1), NA_WIN_C - 1) + (NA_WIN_C - 1)
    var = jnp.arange(NA_WIN_R)
    dr = jnp.arange(NA_WIN_R)[None, :] - var[:, None] + (NA_WIN_R - 1)
    tab = rpb.astype(F32)[:, dr[:, None, :, None], dc[None, :, None, :]]
    tab = jnp.where(col_in[None, None, :, None, :], tab, NEG_INF)
    return tab.transpose(1, 0, 2, 3, 4).reshape(NA_WIN_R, NA_HEADS, GRID_W, NA_WIN_R * GRID_W)


def neighbourhood_attention(proj, cache_k, cache_v, bias_tab, layer, seq, row0, batch):
    rows_n = seq // GRID_W
    assert rows_n >= NA_WIN_R
    past = cache_k.shape[2]
    qb0 = row0 // GRID_W
    sb0 = row0 // seq

    def variant(r):
        return r - _na_row_start(r, rows_n)

    return pl.pallas_call(
        functools.partial(_nbr_attn_kernel, rows_n=rows_n),
        out_shape=jax.ShapeDtypeStruct((batch * seq, NA_W), BF),
        grid=(batch, rows_n),
        in_specs=[pl.BlockSpec((GRID_W, NA_W), lambda b, r: (qb0 + b * rows_n + r, 0)),
                  pl.BlockSpec((seq, NA_W), lambda b, r: (sb0 + b, 1)),
                  pl.BlockSpec((seq, NA_W), lambda b, r: (sb0 + b, 2)),
                  pl.BlockSpec((None, None, past, NA_W), lambda b, r: (b, layer, 0, 0)),
                  pl.BlockSpec((None, None, past, NA_W), lambda b, r: (b, layer, 0, 0)),
                  pl.BlockSpec((None, NA_HEADS, GRID_W, NA_WIN_R * GRID_W), lambda b, r: (variant(r), 0, 0, 0))],
        out_specs=pl.BlockSpec((GRID_W, NA_W), lambda b, r: (b * rows_n + r, 0)),
        compiler_params=_params(("parallel", "arbitrary")),
        name="neighbourhood_attention",
    )(proj, proj, proj, cache_k, cache_v, bias_tab)


def _merge_kernel(oa, ob, oc, od, g0, g1, g2, g3, wb, o_ref):
    acc = None
    for n, (o_n, g_n) in enumerate(((oa, g0), (ob, g1), (oc, g2), (od, g3))):
        term = _sigmoid(g_n[...].astype(F32)) * jnp.dot(o_n[...], wb[n], preferred_element_type=F32)
        acc = term if acc is None else acc + term
    o_ref[...] = acc.astype(o_ref.dtype)


def branch_merge(branches, gate_proj, w_branch):
    t = gate_proj.shape[0]
    tm, tn = 512, 1024
    nj = D_MODEL // tn
    return pl.pallas_call(
        _merge_kernel,
        out_shape=jax.ShapeDtypeStruct((t, D_MODEL), BF),
        grid=(t // tm, nj),
        in_specs=([pl.BlockSpec((tm, BRANCH_W), lambda i, j: (i, 0)) for _ in range(N_BRANCH)]
                  + [pl.BlockSpec((tm, tn), lambda i, j, n=n: (i, n * nj + j)) for n in range(N_BRANCH)]
                  + [pl.BlockSpec((N_BRANCH, BRANCH_W, tn), lambda i, j: (0, 0, j))]),
        out_specs=pl.BlockSpec((tm, tn), lambda i, j: (i, j)),
        compiler_params=_params(("parallel", "arbitrary")),
        name="branch_merge",
    )(*branches, gate_proj, gate_proj, gate_proj, gate_proj, w_branch)


EXP_PER_GROUP = N_EXPERTS // N_EXP_GROUPS


def _first_argmax(v, idx, size):
    m = jnp.max(v, axis=0, keepdims=True)
    first = jnp.min(jnp.where(v == m, idx, size), axis=0, keepdims=True)
    return m, first


def _router_kernel(h_ref, rw_ref, b_ref, e_ref, w_ref):
    logits = lax.dot_general(rw_ref[...], h_ref[...], (((1,), (1,)), ((), ())), preferred_element_type=F32,
                             precision=lax.Precision.HIGHEST)
    scores = _sigmoid(logits)
    biased = scores + b_ref[...]
    tm = scores.shape[1]
    sub = lax.broadcasted_iota(jnp.int32, (EXP_PER_GROUP, tm), 0)
    grp = []
    for g in range(N_EXP_GROUPS):
        v = biased[g * EXP_PER_GROUP:(g + 1) * EXP_PER_GROUP]
        m1, first = _first_argmax(v, sub, EXP_PER_GROUP)
        m2 = jnp.max(jnp.where(sub == first, -jnp.inf, v), axis=0, keepdims=True)
        grp.append(m1 + m2)
    cur = jnp.concatenate(grp, axis=0)
    gidx = lax.broadcasted_iota(jnp.int32, (N_EXP_GROUPS, tm), 0)
    sel = jnp.zeros((N_EXP_GROUPS, tm), F32)
    for _ in range(TOPK_GROUPS):
        _, first = _first_argmax(cur, gidx, N_EXP_GROUPS)
        pick = gidx == first
        sel = jnp.where(pick, 1.0, sel)
        cur = jnp.where(pick, -jnp.inf, cur)
    cur = jnp.concatenate(
        [jnp.where(sel[g:g + 1] > 0.5, biased[g * EXP_PER_GROUP:(g + 1) * EXP_PER_GROUP], NEG_INF)
         for g in range(N_EXP_GROUPS)], axis=0)
    eidx = lax.broadcasted_iota(jnp.int32, (N_EXPERTS, tm), 0)
    es, ws = [], []
    for _ in range(TOP_K):
        _, first = _first_argmax(cur, eidx, N_EXPERTS)
        pick = eidx == first
        es.append(first)
        ws.append(jnp.sum(jnp.where(pick, scores, 0.0), axis=0, keepdims=True))
        cur = jnp.where(pick, -jnp.inf, cur)
    total = functools.reduce(lambda a, b: a + b, ws)
    pad = 8 - TOP_K
    e_ref[...] = jnp.concatenate(es + [jnp.zeros((pad, tm), jnp.int32)], axis=0)
    w_ref[...] = jnp.concatenate([w / total * ROUTED_SCALE for w in ws] + [jnp.zeros((pad, tm), F32)], axis=0)


def moe_route(h, router_w, router_bias):
    t = h.shape[0]
    tm = 512
    return pl.pallas_call(
        _router_kernel,
        out_shape=(jax.ShapeDtypeStruct((8, t), jnp.int32), jax.ShapeDtypeStruct((8, t), F32)),
        grid=(t // tm,),
        in_specs=[pl.BlockSpec((tm, D_MODEL), lambda i: (i, 0)),
                  pl.BlockSpec((N_EXPERTS, D_MODEL), lambda i: (0, 0)),
                  pl.BlockSpec((N_EXPERTS, 1), lambda i: (0, 0))],
        out_specs=(pl.BlockSpec((8, tm), lambda i: (0, i)), pl.BlockSpec((8, tm), lambda i: (0, i))),
        compiler_params=_params(("parallel",)),
        name="moe_route",
    )(h, router_w.T.astype(F32), router_bias.reshape(N_EXPERTS, 1).astype(F32))


GATHER_ROWS = 1024


def _gather_kernel(idx_hbm, src, dst, idx_smem, sem_idx, sem_rows):
    i = pl.program_id(0)
    load = pltpu.make_async_copy(idx_hbm.at[i], idx_smem, sem_idx)
    load.start()
    load.wait()

    def issue(r, carry):
        pltpu.make_async_copy(src.at[pl.ds(idx_smem[r], 1)], dst.at[pl.ds(r, 1)], sem_rows).start()
        return carry

    lax.fori_loop(0, GATHER_ROWS, issue, 0, unroll=8)
    pltpu.make_async_copy(src.at[pl.ds(0, GATHER_ROWS)], dst, sem_rows).wait()


def gather_rows(src, idx):
    n_out = idx.shape[0]
    steps = n_out // GATHER_ROWS
    return pl.pallas_call(
        _gather_kernel,
        out_shape=jax.ShapeDtypeStruct((n_out,) + src.shape[1:], src.dtype),
        grid=(steps,),
        in_specs=[pl.BlockSpec(memory_space=pl.ANY), pl.BlockSpec(memory_space=pl.ANY)],
        out_specs=pl.BlockSpec((GATHER_ROWS,) + src.shape[1:], lambda i: (i, 0, 0)),
        scratch_shapes=[pltpu.SMEM((GATHER_ROWS,), jnp.int32), pltpu.SemaphoreType.DMA, pltpu.SemaphoreType.DMA],
        compiler_params=_params(("arbitrary",)),
        name="gather_rows",
    )(idx.reshape(steps, GATHER_ROWS), src)


def _expert_kernel(blk_e, blk_valid, x_ref, w1, w3, w2, y_ref):
    i = pl.program_id(0)

    @pl.when(blk_valid[i] > 0)
    def _():
        x = _from_slabs(x_ref).astype(BF)
        a = _silu(jnp.dot(x, w1[...], preferred_element_type=F32)) * jnp.dot(x, w3[...], preferred_element_type=F32)
        _to_slabs(y_ref, jnp.dot(a.astype(BF), w2[...], preferred_element_type=F32))

    @pl.when(blk_valid[i] == 0)
    def _():
        y_ref[...] = jnp.zeros(y_ref.shape, y_ref.dtype)


def expert_ffn(xs, blk_e, blk_valid, w1, w3, w2):
    cap = xs.shape[0]
    bm = MOE_BLOCK_ROWS
    slab = pl.BlockSpec((bm, ROW_SLABS, LANES), lambda i, be, bv: (i, 0, 0))
    return pl.pallas_call(
        _expert_kernel,
        out_shape=jax.ShapeDtypeStruct((cap, ROW_SLABS, LANES), F32),
        grid_spec=pltpu.PrefetchScalarGridSpec(
            num_scalar_prefetch=2,
            grid=(cap // bm,),
            in_specs=[slab,
                      pl.BlockSpec((None, D_MODEL, EXPERT_FF), lambda i, be, bv: (be[i], 0, 0)),
                      pl.BlockSpec((None, D_MODEL, EXPERT_FF), lambda i, be, bv: (be[i], 0, 0)),
                      pl.BlockSpec((None, EXPERT_FF, D_MODEL), lambda i, be, bv: (be[i], 0, 0))],
            out_specs=slab),
        compiler_params=_params(("arbitrary",)),
        name="expert_ffn",
    )(blk_e, blk_valid, xs, w1, w3, w2)


def _combine_kernel(x_ref, h_ref, y0, y1, y2, y3, y4, y5, wt_ref, sw1, sw3, sw2, g_ref, o_ref):
    h = h_ref[...].astype(BF)
    a = _silu(jnp.dot(h, sw1[...], preferred_element_type=F32)) * jnp.dot(h, sw3[...], preferred_element_type=F32)
    shared = jnp.dot(a.astype(BF), sw2[...], preferred_element_type=F32)
    routed = None
    for k, y in enumerate((y0, y1, y2, y3, y4, y5)):
        term = _from_slabs(y) * wt_ref[:, k:k + 1]
        routed = term if routed is None else routed + term
    o_ref[...] = x_ref[...] + g_ref[...] * (routed + shared)


def moe_combine(x, h, yg, wts_t, sw1, sw3, sw2, gate, n_prompt):
    t = x.shape[0]
    tm = 256
    nt = t // tm
    row = functools.partial(_mod_row, tm=tm, n_prompt=n_prompt)
    tile = pl.BlockSpec((tm, D_MODEL), lambda i: (i, 0))
    return pl.pallas_call(
        _combine_kernel,
        out_shape=jax.ShapeDtypeStruct((t, D_MODEL), F32),
        grid=(nt,),
        in_specs=([tile, tile]
                  + [pl.BlockSpec((tm, ROW_SLABS, LANES), lambda i, k=k: (k * nt + i, 0, 0)) for k in range(TOP_K)]
                  + [pl.BlockSpec((tm, 8), lambda i: (i, 0)),
                     pl.BlockSpec((D_MODEL, EXPERT_FF), lambda i: (0, 0)),
                     pl.BlockSpec((D_MODEL, EXPERT_FF), lambda i: (0, 0)),
                     pl.BlockSpec((EXPERT_FF, D_MODEL), lambda i: (0, 0)),
                     pl.BlockSpec((None, 1, D_MODEL), lambda i: (row(i), 0, 0))]),
        out_specs=tile,
        compiler_params=_params(("parallel",)),
        name="moe_combine",
    )(x, h, yg, yg, yg, yg, yg, yg, wts_t, sw1, sw3, sw2, gate)


def moe_dispatch_plan(eidx):
    t_n = eidx.shape[0]
    bm = MOE_BLOCK_ROWS
    n_pairs = t_n * TOP_K
    n_blocks = n_pairs // bm + N_EXPERTS
    flat_e = eidx.reshape(-1)
    counts = jnp.zeros((N_EXPERTS,), jnp.int32).at[flat_e].add(1)
    cnt_start = jnp.cumsum(counts) - counts
    pad_counts = (counts + bm - 1) // bm * bm
    pad_end = jnp.cumsum(pad_counts)
    pad_start = pad_end - pad_counts
    order = jnp.argsort(flat_e, stable=True).astype(jnp.int32)
    se = flat_e[order]
    dest = pad_start[se] + jnp.arange(n_pairs, dtype=jnp.int32) - cnt_start[se]
    row_tok = jnp.zeros((n_blocks * bm,), jnp.int32).at[dest].set(order // TOP_K)
    pos = jnp.zeros((n_pairs,), jnp.int32).at[order].set(dest)
    blk_start = jnp.arange(n_blocks, dtype=jnp.int32) * bm
    blk_e = jnp.minimum(jnp.searchsorted(pad_end, blk_start, side="right"), N_EXPERTS - 1).astype(jnp.int32)
    blk_valid = (blk_start < pad_end[-1]).astype(jnp.int32)
    return row_tok, pos, blk_e, blk_valid


def moe_ffn(x, h, h_slabs, prm, layer, gate, n_prompt):
    t_n = h.shape[0]
    e8, w8 = moe_route(h, prm["router_w"][layer], prm["router_bias"][layer])
    row_tok, pos, blk_e, blk_valid = moe_dispatch_plan(e8[:TOP_K].T)
    xs = gather_rows(h_slabs, row_tok)
    ys = expert_ffn(xs, blk_e, blk_valid, prm["exp_w1"][layer].astype(BF), prm["exp_w3"][layer].astype(BF),
                    prm["exp_w2"][layer].astype(BF))
    yg = gather_rows(ys, pos.reshape(t_n, TOP_K).T.reshape(-1))
    return moe_combine(x, h, yg, w8.T, prm["sh_w1"][layer].astype(BF), prm["sh_w3"][layer].astype(BF),
                       prm["sh_w2"][layer].astype(BF), gate, n_prompt)


def _split_w_in(w):
    offs = np.concatenate([[0], np.cumsum(IN_SPLITS)])
    dn_qkv, dn_z, dn_b, dn_a, ssd_z, ssd_xbc, ssd_dt, hy_u, na_qkv, gates = [
        w[:, int(offs[i]):int(offs[i + 1])] for i in range(len(IN_SPLITS))]
    small = jnp.concatenate([dn_b, dn_a, ssd_dt], axis=1)
    small = jnp.pad(small, ((0, 0), (0, LANES - small.shape[1])))
    cast = lambda a: a.astype(BF)
    return dict(dn=cast(jnp.concatenate([dn_qkv, dn_z], axis=1)), ssd=cast(jnp.concatenate([ssd_xbc, ssd_z], axis=1)),
                hy=cast(hy_u), na=cast(na_qkv), gates=cast(gates), small=cast(small))


def kernel(x_prompt, x_sample, cache_k, cache_v, state_delta, state_ssd, c, c_ctx,
           ada_w, ada_b, norm_w, final_norm_w, w_in,
           dn_conv_w, dn_conv_b, dn_A_log, dn_dt_bias, dn_norm_w,
           ssd_conv_w, ssd_conv_b, ssd_A_log, ssd_dt_bias, ssd_D, ssd_norm_w,
           hy_conv_w, hy_conv_b, hy_f1_w, hy_f1_b, hy_f2_w, hy_f2_b, hy_f3_w, hy_freq, hy_decay, hy_bias,
           na_rpb, w_branch, w_out,
           router_w, router_bias, exp_w1, exp_w3, exp_w2, sh_w1, sh_w3, sh_w2):
    prm = dict(hy_conv_w=hy_conv_w, hy_conv_b=hy_conv_b, hy_f1_w=hy_f1_w, hy_f1_b=hy_f1_b, hy_f2_w=hy_f2_w,
               hy_f2_b=hy_f2_b, hy_f3_w=hy_f3_w, hy_freq=hy_freq, hy_decay=hy_decay, hy_bias=hy_bias,
               router_w=router_w, router_bias=router_bias, exp_w1=exp_w1, exp_w3=exp_w3, exp_w2=exp_w2,
               sh_w1=sh_w1, sh_w3=sh_w3, sh_w2=sh_w2)
    bp, sp, d = x_prompt.shape
    bs, ss, _ = x_sample.shape
    n_prompt = bp * sp
    x = jnp.concatenate([x_prompt.reshape(n_prompt, d), x_sample.reshape(bs * ss, d)], axis=0)
    t_n = x.shape[0]
    cvecs = jnp.zeros((8, d), F32).at[0].set(c_ctx).at[1:1 + bs].set(c)
    mod = ada_modulation(cvecs, ada_w, ada_b).reshape(DEPTH, 8, 6, 1, d)
    ck = cache_k.reshape(bs, DEPTH, cache_k.shape[2], NA_W)
    cv = cache_v.reshape(bs, DEPTH, cache_v.shape[2], NA_W)
    sd0 = state_delta
    ss0 = state_ssd.reshape(bs, DEPTH, 2, SSD_XW, SSD_N)
    passes = ((sp, 0, bp, True), (ss, n_prompt, bs, False))
    new_k, new_v, new_dn, new_ssd = [], [], [], []
    for l in range(DEPTH):
        sh1, sc1, g1, sh2, sc2, g2 = [mod[l, :, n] for n in range(6)]
        h = norm_modulate(x, norm_w[l, 0], sc1, sh1, n_prompt, BF)
        w = _split_w_in(w_in[l])
        dn_proj = matmul(h, w["dn"], BF)
        ssd_proj = matmul(h, w["ssd"], BF)
        hy_proj = matmul(h, w["hy"], BF)
        na_proj = matmul(h, w["na"], BF)
        gate_proj = matmul(h, w["gates"], BF, tn=1024)
        pcol = matmul(h, w["small"], F32)
        prow = pcol.reshape(t_n // CHUNK, CHUNK, LANES).transpose(0, 2, 1)
        pc_par, pr_par = gate_params(dn_A_log[l], dn_dt_bias[l], ssd_A_log[l], ssd_dt_bias[l])
        bias_tab = neighbourhood_bias_table(na_rpb[l])
        o_a, o_b, o_c, o_d = [], [], [], []
        for seq, row0, batch, is_ctx in passes:
            qkvc = conv_silu(dn_proj, dn_conv_w[l], dn_conv_b[l], DN_QKV, seq, row0, batch)
            of, ob, dn_fin = deltanet_scan(qkvc, pcol, prow, pc_par, pr_par, None if is_ctx else sd0,
                                           l, seq, row0, batch)
            o_a.append(deltanet_out(of, ob, dn_proj, dn_norm_w[l], row0))
            xbcc = conv_silu(ssd_proj, ssd_conv_w[l], ssd_conv_b[l], SSD_XBC, seq, row0, batch)
            yf, yb, ssd_fin = ssd_scan(xbcc, pcol, prow, pc_par, pr_par, None if is_ctx else ss0,
                                       l, seq, row0, batch)
            o_b.append(ssd_out(yf, yb, xbcc, ssd_proj, ssd_D[l], ssd_norm_w[l], row0))
            o_c.append(hyena_mixer(hy_proj, prm, l, seq, row0, batch))
            if is_ctx:
                o_d.append(context_attention(na_proj, seq, batch))
                new_k.append(na_proj[:n_prompt, NA_W:2 * NA_W].astype(F32).reshape(bp, sp, NA_HEADS, NA_HD))
                new_v.append(na_proj[:n_prompt, 2 * NA_W:].astype(F32).reshape(bp, sp, NA_HEADS, NA_HD))
                new_dn.append(dn_fin)
                new_ssd.append(ssd_fin.reshape(bp, 2, SSD_HEADS, SSD_P, SSD_N))
            else:
                o_d.append(neighbourhood_attention(na_proj, ck, cv, bias_tab, l, seq, row0, batch))
        branches = [jnp.concatenate(o, axis=0) for o in (o_a, o_b, o_c, o_d)]
        merged = branch_merge(branches, gate_proj, w_branch[l].astype(BF))
        x = matmul_residual(merged, w_out[l].astype(BF), x, g1, n_prompt)
        h2, h2_slabs = norm_modulate(x, norm_w[l, 1], sc2, sh2, n_prompt, F32, with_slabs=True)
        x = moe_ffn(x, h2, h2_slabs, prm, l, g2, n_prompt)
    y = final_norm(x, final_norm_w)
    return (y[:n_prompt].reshape(bp, sp, d), y[n_prompt:].reshape(bs, ss, d),
            jnp.stack(new_k, axis=1), jnp.stack(new_v, axis=1), jnp.stack(new_dn, axis=1), jnp.stack(new_ssd, axis=1))
```
